```python
import math
import jax, jax.numpy as jnp
from jax import lax
import numpy as np

D_MODEL = 2048
BATCH = 2
SEQ = 16384
DEPTH = 1

ATTN_WIDTH = D_MODEL // 2
ATTN_HEAD_DIM = 128
ATTN_HEADS = ATTN_WIDTH // ATTN_HEAD_DIM
DILATED_BRANCHES = ((128, 1), (512, 4), (2048, 16))
ROPE_THETA = 500000.0
ROPE_DIM = ATTN_HEAD_DIM // 4
HG_WIDTH = D_MODEL - ATTN_WIDTH
HG_EXPAND = 128
HG_HEADS = HG_WIDTH // HG_EXPAND
HG_CHUNK = 64
MIX_WIDTH = ATTN_WIDTH + HG_WIDTH
IN_COLS = 3 * ATTN_WIDTH + 4 * HG_WIDTH
D_FF = 4 * D_MODEL
N_MOD = 6
EPS = 1e-6

kernel_name = 'hybrid_dilated_attn_hgrn2_block'


def rms_norm(x, w):
    xf = x.astype(jnp.float32)
    y = xf * lax.rsqrt(jnp.mean(xf * xf, axis=-1, keepdims=True) + EPS)
    return y * w.astype(jnp.float32)


def rope_partial(x, pos):
    half = ROPE_DIM // 2
    inv_freq = ROPE_THETA ** (-(jnp.arange(half, dtype=jnp.float32) * 2.0) / ROPE_DIM)
    ang = pos.astype(jnp.float32)[..., None] * inv_freq
    cos = jnp.cos(ang)[:, :, None, :]
    sin = jnp.sin(ang)[:, :, None, :]
    x1 = x[..., :half]
    x2 = x[..., half:ROPE_DIM]
    return jnp.concatenate([x1 * cos - x2 * sin, x2 * cos + x1 * sin, x[..., ROPE_DIM:]], axis=-1)


def dilated_branch(q, k, v, window, dilation):
    blk = window // dilation
    B, S, H, Dh = q.shape
    span = blk * dilation
    L = -(-S // span) * span
    M = L // dilation
    nb = M // blk

    def to_blocks(t):
        t = jnp.pad(t, ((0, 0), (0, L - S), (0, 0), (0, 0)))
        t = t.reshape(B, M, dilation, H, Dh).transpose(0, 2, 3, 1, 4)
        return t.reshape(B, dilation, H, nb, blk, Dh)

    def with_prev(t):
        prev = jnp.pad(t[:, :, :, :-1], ((0, 0), (0, 0), (0, 0), (1, 0), (0, 0), (0, 0)))
        return jnp.concatenate([prev, t], axis=4)

    qb = to_blocks(q)
    kk = with_prev(to_blocks(k))
    vv = with_prev(to_blocks(v))
    s = jnp.einsum('brhnqd,brhnkd->brhnqk', qb, kk) / math.sqrt(Dh)
    i = jnp.arange(blk)[:, None]
    j = jnp.arange(2 * blk)[None, :]
    dist = blk + i - j
    band = (dist >= 0) & (dist <= blk)
    not_before_start = (jnp.arange(nb)[:, None, None] > 0) | (j[None] >= blk)
    mask = band[None] & not_before_start
    s = jnp.where(mask, s, -jnp.inf)
    m = jnp.max(s, axis=-1, keepdims=True)
    p = jnp.exp(s - m)
    l = jnp.sum(p, axis=-1, keepdims=True)
    o = jnp.einsum('brhnqk,brhnkd->brhnqd', p, vv) / l
    lse = (m + jnp.log(l))[..., 0]
    o = o.reshape(B, dilation, H, M, Dh).transpose(0, 3, 1, 2, 4).reshape(B, L, H, Dh)[:, :S]
    lse = lse.reshape(B, dilation, H, M).transpose(0, 3, 1, 2).reshape(B, L, H)[:, :S]
    return o, lse


def dilated_attention(q, k, v):
    outs, lses = [], []
    for window, dilation in DILATED_BRANCHES:
        o, lse = dilated_branch(q, k, v, window, dilation)
        outs.append(o)
        lses.append(lse)
    w = jax.nn.softmax(jnp.stack(lses, axis=0), axis=0)
    return jnp.sum(w[..., None] * jnp.stack(outs, axis=0), axis=0)


def hgrn2(q, f_logit, inp, lb):
    B, S, _ = q.shape
    C, Hh, dk = HG_CHUNK, HG_HEADS, HG_EXPAND
    f = lb + (1.0 - lb) * jax.nn.sigmoid(f_logit)
    logf = jnp.log(f)
    key = 1.0 - f
    qf = jax.nn.silu(q) * (dk ** -0.5)

    def chunks(t):
        return t.reshape(B, S // C, C, Hh, dk).transpose(1, 0, 3, 2, 4)

    causal = jnp.tril(jnp.ones((C, C), dtype=bool))[..., None]

    def step(state, xs):
        qc, kc, vc, lfc = xs
        b = jnp.cumsum(lfc, axis=2)
        o_inter = jnp.einsum('bhck,bhkv->bhcv', qc * jnp.exp(b), state)
        diff = b[:, :, :, None, :] - b[:, :, None, :, :]
        decay = jnp.exp(jnp.where(causal, diff, -jnp.inf))
        a = jnp.einsum('bhtk,bhsk,bhtsk->bhts', qc, kc, decay)
        o = o_inter + jnp.einsum('bhts,bhsv->bhtv', a, vc)
        b_last = b[:, :, -1:, :]
        new_state = jnp.exp(b_last[:, :, 0, :])[..., None] * state + jnp.einsum(
            'bhsk,bhsv->bhkv', kc * jnp.exp(b_last - b), vc)
        return new_state, o

    state0 = jnp.zeros((B, Hh, dk, dk), dtype=jnp.float32)
    _, o = lax.scan(step, state0, (chunks(qf), chunks(key), chunks(inp), chunks(logf)))
    return o.transpose(1, 0, 3, 2, 4).reshape(B, S, Hh, dk)


def setup_inputs(seed: int = 0) -> dict:
    key = jax.random.key(seed)
    ks = jax.random.split(key, 20)
    f32 = jnp.float32
    nrm = lambda k, shape, scale: jax.random.normal(k, shape, f32) * scale
    gain = lambda k, shape: 1.0 + 0.05 * jax.random.normal(k, shape, f32)
    x = jax.random.normal(ks[0], (BATCH, SEQ, D_MODEL), f32)
    c = jax.random.normal(ks[1], (BATCH, D_MODEL), f32)
    positions = (jnp.arange(SEQ, dtype=jnp.int32)[None, :]
                 + jax.random.randint(ks[2], (BATCH, 1), 0, 4096, dtype=jnp.int32))
    return {
        'x': x,
        'c': c,
        'positions': positions,
        'norm1_w': gain(ks[3], (DEPTH, D_MODEL)),
        'w_ada': nrm(ks[4], (DEPTH, D_MODEL, N_MOD * D_MODEL), 0.5 * D_MODEL ** -0.5),
        'b_ada': nrm(ks[5], (DEPTH, N_MOD * D_MODEL), 0.02),
        'w_in': nrm(ks[6], (DEPTH, D_MODEL, IN_COLS), D_MODEL ** -0.5),
        'q_norm_w': gain(ks[7], (DEPTH, ATTN_HEAD_DIM)),
        'k_norm_w': gain(ks[8], (DEPTH, ATTN_HEAD_DIM)),
        'attn_out_norm_w': gain(ks[9], (DEPTH, ATTN_WIDTH)),
        'hg_lb_logits': nrm(ks[10], (DEPTH + 1, HG_WIDTH), 0.5),
        'hg_norm_w': gain(ks[11], (DEPTH, HG_EXPAND)),
        'w_out': nrm(ks[12], (DEPTH, MIX_WIDTH, D_MODEL), MIX_WIDTH ** -0.5),
        'norm2_w': gain(ks[13], (DEPTH, D_MODEL)),
        'w_ff1': nrm(ks[14], (DEPTH, D_MODEL, D_FF), D_MODEL ** -0.5),
        'w_ff2': nrm(ks[15], (DEPTH, D_FF, D_MODEL), D_FF ** -0.5),
    }


def reference(x, c, positions, norm1_w, w_ada, b_ada, w_in, q_norm_w, k_norm_w, attn_out_norm_w,
              hg_lb_logits, hg_norm_w, w_out, norm2_w, w_ff1, w_ff2):
    dt = x.dtype
    B, S, _ = x.shape
    lb_all = jnp.cumsum(jax.nn.softmax(hg_lb_logits.astype(jnp.float32), axis=0), axis=0)
    split_at = [ATTN_WIDTH, 2 * ATTN_WIDTH, 3 * ATTN_WIDTH, 3 * ATTN_WIDTH + HG_WIDTH,
                3 * ATTN_WIDTH + 2 * HG_WIDTH, 3 * ATTN_WIDTH + 3 * HG_WIDTH]
    for l in range(DEPTH):
        mod = (jax.nn.silu(c) @ w_ada[l] + b_ada[l]).astype(jnp.float32)
        shift1, scale1, gate1, shift2, scale2, gate2 = [m[:, None, :] for m in jnp.split(mod, N_MOD, axis=-1)]

        h = rms_norm(x, norm1_w[l]) * (1.0 + scale1) + shift1
        proj = (h.astype(dt) @ w_in[l]).astype(jnp.float32)
        aq, ak, av, gq, gf, gi, gg = jnp.split(proj, split_at, axis=-1)

        hs = (B, S, ATTN_HEADS, ATTN_HEAD_DIM)
        aq = rope_partial(rms_norm(aq.reshape(hs), q_norm_w[l]), positions)
        ak = rope_partial(rms_norm(ak.reshape(hs), k_norm_w[l]), positions)
        attn = dilated_attention(aq, ak, av.reshape(hs)).reshape(B, S, ATTN_WIDTH)
        attn = rms_norm(attn, attn_out_norm_w[l])

        ho = hgrn2(gq, gf, gi, lb_all[l])
        hg = rms_norm(ho, hg_norm_w[l]) * jax.nn.silu(gg.reshape(B, S, HG_HEADS, HG_EXPAND))
        hg = hg.reshape(B, S, HG_WIDTH)

        mix = jnp.concatenate([attn, hg], axis=-1).astype(dt) @ w_out[l]
        x = x + (gate1 * mix.astype(jnp.float32)).astype(dt)

        h2 = rms_norm(x, norm2_w[l]) * (1.0 + scale2) + shift2
        ff = jnp.square(jax.nn.relu(h2.astype(dt) @ w_ff1[l])) @ w_ff2[l]
        x = x + (gate2 * ff.astype(jnp.float32)).astype(dt)
    return x
```

```python
import functools
import math

import jax
import jax.numpy as jnp
from jax import lax
from jax.experimental import pallas as pl
from jax.experimental.pallas import tpu as pltpu

F32 = jnp.float32
BF16 = jnp.bfloat16

HEAD_DIM = 128
N_HEADS = 8
GROUP_COLS = N_HEADS * HEAD_DIM
ROPE_THETA = 500000.0
ROPE_DIM = HEAD_DIM // 4
ROPE_HALF = ROPE_DIM // 2
DILATIONS = (1, 4, 16)
ATTN_BLK = 128
SPAN = ATTN_BLK * DILATIONS[-1]
HG_CHUNK = 128
HG_SUB = 8
EPS = 1e-6
VMEM_LIMIT = 56 * 1024 * 1024

_N_GROUPS = 7
_SLOT_Q, _SLOT_K, _SLOT_V, _SLOT_HQ, _SLOT_HI, _SLOT_HG = range(6)


def _params(sem, vmem=VMEM_LIMIT):
    return pltpu.CompilerParams(dimension_semantics=sem, vmem_limit_bytes=vmem)


def _silu(x):
    return x * (1.0 / (1.0 + jnp.exp(-x)))


def _ada_kernel(c_ref, w_ref, b_ref, o_ref):
    c = c_ref[...]
    sc = _silu(c).astype(BF16)
    o_ref[...] = jnp.dot(sc, w_ref[...].astype(BF16), preferred_element_type=F32) + b_ref[...]


def _ada_mod(c, w_ada, b_ada):
    bsz, d = c.shape
    n = w_ada.shape[1]
    tn = 1024
    rows = 8
    c8 = jnp.pad(c, ((0, rows - bsz), (0, 0)))
    out = pl.pallas_call(
        _ada_kernel,
        grid=(n // tn,),
        in_specs=[pl.BlockSpec((rows, d), lambda j: (0, 0)),
                  pl.BlockSpec((d, tn), lambda j: (0, j)),
                  pl.BlockSpec((1, tn), lambda j: (0, j))],
        out_specs=pl.BlockSpec((rows, tn), lambda j: (0, j)),
        out_shape=jax.ShapeDtypeStruct((rows, n), F32),
        compiler_params=_params(("arbitrary",)),
        name="ada_mod",
    )(c8, w_ada, b_ada.reshape(1, n))
    return out[:bsz]


def _rope_kernel(pos_ref, invf_ref, c_ref, s1_ref, s2_ref):
    ang = pos_ref[...] * invf_ref[...]
    cos = jnp.cos(ang)
    sin = jnp.sin(ang)
    lane = lax.broadcasted_iota(jnp.int32, ang.shape, 1)
    c_ref[...] = jnp.where(lane < ROPE_DIM, cos, 1.0)
    s1_ref[...] = jnp.where(lane < ROPE_HALF, -sin, 0.0)
    s2_ref[...] = jnp.where((lane >= ROPE_HALF) & (lane < ROPE_DIM), sin, 0.0)


def _rope_tables(positions):
    t = positions.size
    ts = 2048
    inv_freq = ROPE_THETA ** (-(jnp.arange(ROPE_HALF, dtype=F32) * 2.0) / ROPE_DIM)
    invf = jnp.concatenate([inv_freq, inv_freq, jnp.zeros((HEAD_DIM - ROPE_DIM,), F32)]).reshape(1, HEAD_DIM)
    pos = positions.astype(F32).reshape(t, 1)
    tab = jax.ShapeDtypeStruct((t, HEAD_DIM), F32)
    spec = pl.BlockSpec((ts, HEAD_DIM), lambda i: (i, 0))
    return pl.pallas_call(
        _rope_kernel,
        grid=(t // ts,),
        in_specs=[pl.BlockSpec((ts, 1), lambda i: (i, 0)),
                  pl.BlockSpec((1, HEAD_DIM), lambda i: (0, 0))],
        out_specs=[spec, spec, spec],
        out_shape=[tab, tab, tab],
        compiler_params=_params(("arbitrary",)),
        name="rope_tables",
    )(pos, invf)


def _rms_rows(x, eps=EPS):
    return x * lax.rsqrt(jnp.mean(x * x, axis=-1, keepdims=True) + eps)


def _inproj_kernel(x_ref, mod_ref, n1w_ref, w_ref, qnw_ref, knw_ref, c_ref, s1_ref, s2_ref,
                   main_ref, gf_ref, h_scr):
    j = pl.program_id(1)

    @pl.when(j == 0)
    def _():
        y = _rms_rows(x_ref[...]) * n1w_ref[...]
        shift = mod_ref[0, 0:1, :]
        scale = mod_ref[0, 1:2, :]
        h_scr[...] = (y * (1.0 + scale) + shift).astype(BF16)

    def proj():
        return jnp.dot(h_scr[...], w_ref[...], preferred_element_type=F32)

    @pl.when(j <= _SLOT_K)
    def _():
        res = proj()
        wvec = jnp.where(j == _SLOT_Q, qnw_ref[...] * (HEAD_DIM ** -0.5), knw_ref[...])
        cc, s1, s2 = c_ref[...], s1_ref[...], s2_ref[...]
        for h in range(N_HEADS):
            y = _rms_rows(res[:, h * HEAD_DIM:(h + 1) * HEAD_DIM]) * wvec
            y = y * cc + pltpu.roll(y, HEAD_DIM - ROPE_HALF, 1) * s1 + pltpu.roll(y, ROPE_HALF, 1) * s2
            main_ref[0, 0, h] = y.astype(BF16)

    @pl.when((j == _SLOT_V) | (j == _SLOT_HI))
    def _():
        res = proj()
        for h in range(N_HEADS):
            main_ref[0, 0, h] = res[:, h * HEAD_DIM:(h + 1) * HEAD_DIM].astype(BF16)

    @pl.when(j == _SLOT_HQ)
    def _():
        res = proj()
        for h in range(N_HEADS):
            y = res[:, h * HEAD_DIM:(h + 1) * HEAD_DIM]
            main_ref[0, 0, h] = (_silu(y) * (HEAD_DIM ** -0.5)).astype(BF16)

    @pl.when(j == _SLOT_HG)
    def _():
        res = proj()
        for h in range(N_HEADS):
            main_ref[0, 0, h] = _silu(res[:, h * HEAD_DIM:(h + 1) * HEAD_DIM]).astype(BF16)

    @pl.when(j == _N_GROUPS - 1)
    def _():
        res = proj()
        for h in range(N_HEADS):
            gf_ref[0, h] = res[:, h * HEAD_DIM:(h + 1) * HEAD_DIM]


def _in_projection(x2, mod3, norm1_w, w_in_bf, qnw, knw, tabs, bsz, seq):
    t, d = x2.shape
    tm = 512
    rpb = seq // tm

    def wcol(i, j):
        return (0, jnp.where(j < 4, j, jnp.where(j == _N_GROUPS - 1, 4, j + 1)))

    row = lambda i, j: (i, 0)
    const = lambda i, j: (0, 0)
    main_shape = jax.ShapeDtypeStruct((6, bsz, N_HEADS, seq, HEAD_DIM), BF16)
    gf_shape = jax.ShapeDtypeStruct((bsz, N_HEADS, seq, HEAD_DIM), F32)
    return pl.pallas_call(
        _inproj_kernel,
        grid=(t // tm, _N_GROUPS),
        in_specs=[pl.BlockSpec((tm, d), row),
                  pl.BlockSpec((1, 6, d), lambda i, j: (i // rpb, 0, 0)),
                  pl.BlockSpec((1, d), const),
                  pl.BlockSpec((d, GROUP_COLS), wcol),
                  pl.BlockSpec((1, HEAD_DIM), const),
                  pl.BlockSpec((1, HEAD_DIM), const),
                  pl.BlockSpec((tm, HEAD_DIM), row),
                  pl.BlockSpec((tm, HEAD_DIM), row),
                  pl.BlockSpec((tm, HEAD_DIM), row)],
        out_specs=[pl.BlockSpec((1, 1, N_HEADS, tm, HEAD_DIM),
                                lambda i, j: (jnp.minimum(j, 5), i // rpb, 0, i % rpb, 0)),
                   pl.BlockSpec((1, N_HEADS, tm, HEAD_DIM), lambda i, j: (i // rpb, 0, i % rpb, 0))],
        out_shape=[main_shape, gf_shape],
        scratch_shapes=[pltpu.VMEM((tm, d), BF16)],
        compiler_params=_params(("arbitrary", "arbitrary")),
        name="in_proj",
    )(x2, mod3, norm1_w.reshape(1, d), w_in_bf, qnw.reshape(1, HEAD_DIM), knw.reshape(1, HEAD_DIM), *tabs)


def _band_attention(q, kp, kc, vp, vc, first):
    kk = jnp.concatenate([kp, kc], axis=0)
    vv = jnp.concatenate([vp, vc], axis=0)
    s = lax.dot_general(q, kk, (((1,), (1,)), ((), ())), preferred_element_type=F32)
    qi = lax.broadcasted_iota(jnp.int32, s.shape, 0)
    kj = lax.broadcasted_iota(jnp.int32, s.shape, 1)
    dist = ATTN_BLK + qi - kj
    valid = (dist >= 0) & (dist <= ATTN_BLK) & ((kj >= ATTN_BLK) | jnp.logical_not(first))
    s = jnp.where(valid, s, -jnp.inf)
    m = jnp.max(s, axis=-1, keepdims=True)
    p = jnp.exp(s - m)
    l = jnp.sum(p, axis=-1, keepdims=True)
    o = jnp.dot(p.astype(BF16), vv, preferred_element_type=F32)
    return o / l, m + jnp.log(l)


def _attn_kernel(qn, q4, q16, kn, knp, k4, k4p, k16, k16p, vn, vnp, v4, v4p, v16, v16p,
                 o_ref, o_scr, l_scr):
    first = pl.program_id(2) == 0
    nblk = SPAN // ATTN_BLK

    def put(branch, start, stride, o, lse):
        rows = pl.ds(start, ATTN_BLK, stride=stride) if stride > 1 else pl.ds(start, ATTN_BLK)
        o_scr[branch, rows, :] = o
        l_scr[branch, rows, :] = jnp.broadcast_to(lse, (ATTN_BLK, HEAD_DIM))

    for r in range(nblk):
        lo = r * ATTN_BLK
        q = qn[0, 0, 0, lo:lo + ATTN_BLK, :]
        kc = kn[0, 0, 0, lo:lo + ATTN_BLK, :]
        vc = vn[0, 0, 0, lo:lo + ATTN_BLK, :]
        if r == 0:
            kp, vp, fst = knp[0, 0, 0], vnp[0, 0, 0], first
        else:
            kp = kn[0, 0, 0, lo - ATTN_BLK:lo, :]
            vp = vn[0, 0, 0, lo - ATTN_BLK:lo, :]
            fst = False
        o, lse = _band_attention(q, kp, kc, vp, vc, fst)
        put(0, lo, 1, o, lse)

    d = DILATIONS[1]
    for c in range(d):
        cols = slice(c * HEAD_DIM, (c + 1) * HEAD_DIM)
        for a in range(nblk // d):
            lo = a * ATTN_BLK
            q = q4[0, 0, 0, lo:lo + ATTN_BLK, cols]
            kc = k4[0, 0, 0, lo:lo + ATTN_BLK, cols]
            vc = v4[0, 0, 0, lo:lo + ATTN_BLK, cols]
            if a == 0:
                kp, vp, fst = k4p[0, 0, 0, :, cols], v4p[0, 0, 0, :, cols], first
            else:
                kp = k4[0, 0, 0, lo - ATTN_BLK:lo, cols]
                vp = v4[0, 0, 0, lo - ATTN_BLK:lo, cols]
                fst = False
            o, lse = _band_attention(q, kp, kc, vp, vc, fst)
            put(1, lo * d + c, d, o, lse)

    d = DILATIONS[2]
    for c in range(d):
        cols = slice(c * HEAD_DIM, (c + 1) * HEAD_DIM)
        o, lse = _band_attention(q16[0, 0, 0, :, cols], k16p[0, 0, 0, :, cols], k16[0, 0, 0, :, cols],
                                 v16p[0, 0, 0, :, cols], v16[0, 0, 0, :, cols], first)
        put(2, c, d, o, lse)

    for r in range(nblk):
        rows = slice(r * ATTN_BLK, (r + 1) * ATTN_BLK)
        l0, l1, l2 = l_scr[0, rows, :], l_scr[1, rows, :], l_scr[2, rows, :]
        mx = jnp.maximum(jnp.maximum(l0, l1), l2)
        w0, w1, w2 = jnp.exp(l0 - mx), jnp.exp(l1 - mx), jnp.exp(l2 - mx)
        acc = w0 * o_scr[0, rows, :] + w1 * o_scr[1, rows, :] + w2 * o_scr[2, rows, :]
        o_ref[0, rows, :] = (acc / (w0 + w1 + w2)).astype(o_ref.dtype)


def _dilated_attention(main, bsz, seq):
    nspan = seq // SPAN
    views = {d: main.reshape(6, bsz, N_HEADS, seq // d, d * HEAD_DIM) for d in DILATIONS}
    in_specs, args = [], []

    def add(slot, d, prev):
        rows = SPAN // d
        if prev:
            per = rows // ATTN_BLK
            spec = pl.BlockSpec((1, 1, 1, ATTN_BLK, d * HEAD_DIM),
                                lambda b, h, n, s=slot, p=per: (s, b, h, jnp.maximum(n * p - 1, 0), 0))
        else:
            spec = pl.BlockSpec((1, 1, 1, rows, d * HEAD_DIM), lambda b, h, n, s=slot: (s, b, h, n, 0))
        in_specs.append(spec)
        args.append(views[d])

    for d in DILATIONS:
        add(_SLOT_Q, d, False)
    for slot in (_SLOT_K, _SLOT_V):
        for d in DILATIONS:
            add(slot, d, False)
            add(slot, d, True)

    return pl.pallas_call(
        _attn_kernel,
        grid=(bsz, N_HEADS, nspan),
        in_specs=in_specs,
        out_specs=pl.BlockSpec((1, SPAN, HEAD_DIM), lambda b, h, n: (b, n, h)),
        out_shape=jax.ShapeDtypeStruct((bsz, seq, GROUP_COLS), BF16),
        scratch_shapes=[pltpu.VMEM((3, SPAN, HEAD_DIM), F32), pltpu.VMEM((3, SPAN, HEAD_DIM), F32)],
        compiler_params=_params(("arbitrary", "arbitrary", "arbitrary")),
        name="dilated_attn",
    )(*args)


def _split3(x):
    hi = x.astype(BF16)
    r1 = x - hi.astype(F32)
    mid = r1.astype(BF16)
    lo = (r1 - mid.astype(F32)).astype(BF16)
    return hi, mid, lo


def _hgrn_kernel(q_ref, i_ref, g_ref, f_ref, lbl_ref, nw_ref, o_ref, st_scr):
    c_rows = HG_CHUNK

    @pl.when(pl.program_id(2) == 0)
    def _():
        st_scr[...] = jnp.zeros_like(st_scr)

    lg = lbl_ref[...]
    e = jnp.exp(lg - jnp.max(lg, axis=0, keepdims=True))
    lb = e[0:1, :] / jnp.sum(e, axis=0, keepdims=True)

    ti = lax.broadcasted_iota(jnp.int32, (c_rows, c_rows), 0)
    si = lax.broadcasted_iota(jnp.int32, (c_rows, c_rows), 1)
    tril = (ti >= si).astype(BF16)
    rowid = lax.broadcasted_iota(jnp.int32, (c_rows, HEAD_DIM), 0)
    sub_t = lax.broadcasted_iota(jnp.int32, (HG_SUB, HEAD_DIM), 0)
    sub_lane = lax.broadcasted_iota(jnp.int32, (HG_SUB, HEAD_DIM), 1)
    levels = []
    sz = HG_SUB
    while sz < c_rows:
        levels.append(sz)
        sz *= 2

    def chunk(ci, carry):
        r0 = pl.multiple_of(ci * c_rows, c_rows)
        rows = pl.ds(r0, c_rows)
        g = f_ref[0, 0, rows, :]
        sg = 1.0 / (1.0 + jnp.exp(-g))
        f = lb + (1.0 - lb) * sg
        logf = jnp.log(f)
        key = 1.0 - f
        q = q_ref[0, 0, 0, rows, :].astype(F32)
        v = i_ref[0, 0, 0, rows, :]

        hi, mid, lo = _split3(logf)
        b = (jnp.dot(tril, hi, preferred_element_type=F32) + jnp.dot(tril, mid, preferred_element_type=F32)
             + jnp.dot(tril, lo, preferred_element_type=F32))
        b_last = b[c_rows - 1:c_rows, :]

        st = st_scr[...]
        o = lax.dot_general((q * jnp.exp(b)).astype(BF16), st.astype(BF16), (((1,), (1,)), ((), ())),
                            preferred_element_type=F32)

        a_mat = jnp.zeros((c_rows, c_rows), F32)
        for sz in levels:
            b3 = b.reshape(c_rows // (2 * sz), 2 * sz, HEAD_DIM)
            bref = jnp.broadcast_to(b3[:, sz - 1:sz, :], b3.shape).reshape(c_rows, HEAD_DIM)
            is_r = (rowid & sz) != 0
            dlt = b - bref
            ex = jnp.exp(jnp.where(is_r, dlt, -dlt))
            qh = jnp.where(is_r, q * ex, 0.0).astype(BF16)
            kh = jnp.where(is_r, 0.0, key * ex).astype(BF16)
            al = lax.dot_general(qh, kh, (((1,), (1,)), ((), ())), preferred_element_type=F32)
            a_mat = a_mat + jnp.where((ti ^ si) < 2 * sz, al, 0.0)

        blocks = []
        for i in range(c_rows // HG_SUB):
            rs = slice(i * HG_SUB, (i + 1) * HG_SUB)
            bb, qq, kk = b[rs, :], q[rs, :], key[rs, :]
            ablk = jnp.zeros((HG_SUB, c_rows), F32)
            for s in range(HG_SUB):
                bs = jnp.broadcast_to(bb[s:s + 1, :], bb.shape)
                ks = jnp.broadcast_to(kk[s:s + 1, :], kk.shape)
                p = jnp.where(sub_t >= s, jnp.exp(bb - bs), 0.0) * qq * ks
                a_s = jnp.sum(p, axis=-1, keepdims=True)
                ablk = jnp.where(sub_lane == i * HG_SUB + s, a_s, ablk)
            blocks.append(ablk)
        a_mat = a_mat + jnp.concatenate(blocks, axis=0)

        o = o + jnp.dot(a_mat.astype(BF16), v, preferred_element_type=F32)

        kdec = (key * jnp.exp(b_last - b)).astype(BF16)
        upd = jnp.dot(v.astype(F32).T.astype(BF16), kdec, preferred_element_type=F32)
        st_scr[...] = st * jnp.exp(b_last) + upd

        y = _rms_rows(o) * nw_ref[...]
        o_ref[0, rows, :] = (y * g_ref[0, 0, 0, rows, :].astype(F32)).astype(o_ref.dtype)
        return carry

    lax.fori_loop(0, o_ref.shape[1] // c_rows, chunk, 0)


def _hgrn2(main, gf, lb_logits, hg_norm_w, bsz, seq):
    tt = 2048
    blk = (1, 1, 1, tt, HEAD_DIM)
    spec = lambda slot: pl.BlockSpec(blk, lambda b, h, n, s=slot: (s, b, h, n, 0))
    return pl.pallas_call(
        _hgrn_kernel,
        grid=(bsz, N_HEADS, seq // tt),
        in_specs=[spec(_SLOT_HQ), spec(_SLOT_HI), spec(_SLOT_HG),
                  pl.BlockSpec((1, 1, tt, HEAD_DIM), lambda b, h, n: (b, h, n, 0)),
                  pl.BlockSpec((lb_logits.shape[0], HEAD_DIM), lambda b, h, n: (0, h)),
                  pl.BlockSpec((1, HEAD_DIM), lambda b, h, n: (0, 0))],
        out_specs=pl.BlockSpec((1, tt, HEAD_DIM), lambda b, h, n: (b, n, h)),
        out_shape=jax.ShapeDtypeStruct((bsz, seq, GROUP_COLS), BF16),
        scratch_shapes=[pltpu.VMEM((HEAD_DIM, HEAD_DIM), F32)],
        compiler_params=_params(("arbitrary", "arbitrary", "arbitrary")),
        name="hgrn2",
    )(main, main, main, gf, lb_logits, hg_norm_w.reshape(1, HEAD_DIM))


def _outproj_kernel(x_ref, a_ref, g_ref, mod_ref, aw_ref, w_ref, o_ref):
    half = a_ref.shape[1]
    an = (_rms_rows(a_ref[...].astype(F32)) * aw_ref[...]).astype(BF16)
    mix = (jnp.dot(an, w_ref[0:half, :], preferred_element_type=F32)
           + jnp.dot(g_ref[...], w_ref[half:, :], preferred_element_type=F32))
    o_ref[...] = x_ref[...] + mod_ref[0, 2:3, :] * mix


def _out_projection(x2, attn2, hg2, mod3, attn_norm_w, w_out_bf, seq):
    t, d = x2.shape
    half = attn2.shape[1]
    tm = 512
    rpb = seq // tm
    row = lambda i: (i, 0)
    return pl.pallas_call(
        _outproj_kernel,
        grid=(t // tm,),
        in_specs=[pl.BlockSpec((tm, d), row),
                  pl.BlockSpec((tm, half), row),
                  pl.BlockSpec((tm, half), row),
                  pl.BlockSpec((1, 6, d), lambda i: (i // rpb, 0, 0)),
                  pl.BlockSpec((1, half), lambda i: (0, 0)),
                  pl.BlockSpec((2 * half, d), lambda i: (0, 0))],
        out_specs=pl.BlockSpec((tm, d), row),
        out_shape=jax.ShapeDtypeStruct((t, d), F32),
        compiler_params=_params(("arbitrary",)),
        name="out_proj",
    )(x2, attn2, hg2, mod3, attn_norm_w.reshape(1, half), w_out_bf)


def _mlp_kernel(x_ref, mod_ref, nw_ref, w1_ref, w2_ref, o_ref, h_scr, acc_scr):
    k = pl.program_id(1)

    @pl.when(k == 0)
    def _():
        y = _rms_rows(x_ref[...]) * nw_ref[...]
        h_scr[...] = (y * (1.0 + mod_ref[0, 4:5, :]) + mod_ref[0, 3:4, :]).astype(BF16)

    a = jnp.dot(h_scr[...], w1_ref[...], preferred_element_type=F32)
    a = jnp.square(jnp.maximum(a, 0.0)).astype(BF16)
    part = jnp.dot(a, w2_ref[...], preferred_element_type=F32)

    @pl.when(k == 0)
    def _():
        acc_scr[...] = part

    @pl.when(k > 0)
    def _():
        acc_scr[...] += part

    @pl.when(k == pl.num_programs(1) - 1)
    def _():
        o_ref[...] = x_ref[...] + mod_ref[0, 5:6, :] * acc_scr[...]


def _mlp(x2, mod3, norm2_w, w1_bf, w2_bf, seq):
    t, d = x2.shape
    dff = w1_bf.shape[1]
    tm, tf = 512, 1024
    rpb = seq // tm
    row = lambda i, k: (i, 0)
    return pl.pallas_call(
        _mlp_kernel,
        grid=(t // tm, dff // tf),
        in_specs=[pl.BlockSpec((tm, d), row),
                  pl.BlockSpec((1, 6, d), lambda i, k: (i // rpb, 0, 0)),
                  pl.BlockSpec((1, d), lambda i, k: (0, 0)),
                  pl.BlockSpec((d, tf), lambda i, k: (0, k)),
                  pl.BlockSpec((tf, d), lambda i, k: (k, 0))],
        out_specs=pl.BlockSpec((tm, d), row),
        out_shape=jax.ShapeDtypeStruct((t, d), F32),
        scratch_shapes=[pltpu.VMEM((tm, d), BF16), pltpu.VMEM((tm, d), F32)],
        compiler_params=_params(("arbitrary", "arbitrary")),
        name="mlp",
    )(x2, mod3, norm2_w.reshape(1, d), w1_bf, w2_bf)


def kernel(x, c, positions, norm1_w, w_ada, b_ada, w_in, q_norm_w, k_norm_w, attn_out_norm_w,
           hg_lb_logits, hg_norm_w, w_out, norm2_w, w_ff1, w_ff2):
    bsz, seq, d = x.shape
    assert w_ada.shape[0] == 1, "single-layer block"
    assert seq % SPAN == 0 and d == 2 * GROUP_COLS and w_in.shape[2] == _N_GROUPS * GROUP_COLS
    x2 = x.reshape(bsz * seq, d)

    mod3 = _ada_mod(c, w_ada[0], b_ada[0]).reshape(bsz, 6, d)
    tabs = _rope_tables(positions)

    main, gf = _in_projection(x2, mod3, norm1_w[0], w_in[0].astype(BF16), q_norm_w[0], k_norm_w[0],
                              tabs, bsz, seq)
    attn = _dilated_attention(main, bsz, seq)
    hg = _hgrn2(main, gf, hg_lb_logits, hg_norm_w[0], bsz, seq)

    x1 = _out_projection(x2, attn.reshape(bsz * seq, GROUP_COLS), hg.reshape(bsz * seq, GROUP_COLS), mod3,
                         attn_out_norm_w[0], w_out[0].astype(BF16), seq)
    out = _mlp(x1, mod3, norm2_w[0], w_ff1[0].astype(BF16), w_ff2[0].astype(BF16), seq)
    return out.reshape(bsz, seq, d)
```

```python
import math

import jax
import jax.numpy as jnp
from jax import lax
from jax.experimental import pallas as pl
from jax.experimental.pallas import tpu as pltpu

F32 = jnp.float32
BF16 = jnp.bfloat16

HEAD_DIM = 128
N_HEADS = 8
GROUP_COLS = N_HEADS * HEAD_DIM
ROPE_THETA = 500000.0
ROPE_DIM = HEAD_DIM // 4
ROPE_HALF = ROPE_DIM // 2
DILATIONS = (1, 4, 16)
ATTN_BLK = 128
SPAN = ATTN_BLK * DILATIONS[-1]
PERM_ROWS = 256
HG_CHUNK = 128
HG_SUB = 8
HG_HEADS_PER_STEP = 4
EPS = 1e-6
LOG2E = math.log2(math.e)
VMEM_LIMIT = 56 * 1024 * 1024

_N_GROUPS = 7
_SLOT_Q, _SLOT_K, _SLOT_V, _SLOT_HQ, _SLOT_HI, _SLOT_HG = range(6)


def _params(sem, vmem=VMEM_LIMIT):
    return pltpu.CompilerParams(dimension_semantics=sem, vmem_limit_bytes=vmem)


def _silu(x):
    return x * (1.0 / (1.0 + jnp.exp(-x)))


def _rms_rows(x, eps=EPS):
    return x * lax.rsqrt(jnp.mean(x * x, axis=-1, keepdims=True) + eps)


def _dot_nt(a, b):
    return lax.dot_general(a, b, (((1,), (1,)), ((), ())), preferred_element_type=F32)


def _ada_kernel(c_ref, w_ref, b_ref, o_ref):
    sc = _silu(c_ref[...]).astype(BF16)
    o_ref[...] = jnp.dot(sc, w_ref[...].astype(BF16), preferred_element_type=F32) + b_ref[...]


def _ada_mod(c, w_ada, b_ada):
    bsz, d = c.shape
    n = w_ada.shape[1]
    tn = 1024
    rows = 8
    c8 = jnp.pad(c, ((0, rows - bsz), (0, 0)))
    out = pl.pallas_call(
        _ada_kernel,
        grid=(n // tn,),
        in_specs=[pl.BlockSpec((rows, d), lambda j: (0, 0)),
                  pl.BlockSpec((d, tn), lambda j: (0, j)),
                  pl.BlockSpec((1, tn), lambda j: (0, j))],
        out_specs=pl.BlockSpec((rows, tn), lambda j: (0, j)),
        out_shape=jax.ShapeDtypeStruct((rows, n), F32),
        compiler_params=_params(("arbitrary",)),
        name="ada_mod",
    )(c8, w_ada, b_ada.reshape(1, n))
    return out[:bsz]


def _rope_kernel(pos_ref, invf_ref, c_ref, s1_ref, s2_ref):
    ang = pos_ref[...] * invf_ref[...]
    cos = jnp.cos(ang)
    sin = jnp.sin(ang)
    lane = lax.broadcasted_iota(jnp.int32, ang.shape, 1)
    c_ref[...] = jnp.where(lane < ROPE_DIM, cos, 1.0)
    s1_ref[...] = jnp.where(lane < ROPE_HALF, -sin, 0.0)
    s2_ref[...] = jnp.where((lane >= ROPE_HALF) & (lane < ROPE_DIM), sin, 0.0)


def _rope_tables(positions):
    t = positions.size
    ts = 2048
    inv_freq = ROPE_THETA ** (-(jnp.arange(ROPE_HALF, dtype=F32) * 2.0) / ROPE_DIM)
    invf = jnp.concatenate([inv_freq, inv_freq, jnp.zeros((HEAD_DIM - ROPE_DIM,), F32)]).reshape(1, HEAD_DIM)
    pos = positions.astype(F32).reshape(t, 1)
    tab = jax.ShapeDtypeStruct((t, HEAD_DIM), F32)
    spec = pl.BlockSpec((ts, HEAD_DIM), lambda i: (i, 0))
    return pl.pallas_call(
        _rope_kernel,
        grid=(t // ts,),
        in_specs=[pl.BlockSpec((ts, 1), lambda i: (i, 0)),
                  pl.BlockSpec((1, HEAD_DIM), lambda i: (0, 0))],
        out_specs=[spec, spec, spec],
        out_shape=[tab, tab, tab],
        compiler_params=_params(("arbitrary",)),
        name="rope_tables",
    )(pos, invf)


_PROJ_COLS = 2 * HEAD_DIM


def _inproj_kernel(x_ref, mod_ref, n1w_ref, w_ref, qnw_ref, knw_ref, c_ref, s1_ref, s2_ref,
                   main_ref, gf_ref, h_scr):
    j = pl.program_id(1)

    @pl.when(j == 0)
    def _():
        y = _rms_rows(x_ref[...]) * n1w_ref[...]
        shift = mod_ref[0, 0:1, :]
        scale = mod_ref[0, 1:2, :]
        h_scr[...] = (y * (1.0 + scale) + shift).astype(BF16)

    def heads(fn):
        for hp in range(GROUP_COLS // _PROJ_COLS):
            res = jnp.dot(h_scr[...], w_ref[:, hp * _PROJ_COLS:(hp + 1) * _PROJ_COLS],
                          preferred_element_type=F32)
            for hh in range(_PROJ_COLS // HEAD_DIM):
                fn(hp * (_PROJ_COLS // HEAD_DIM) + hh, res[:, hh * HEAD_DIM:(hh + 1) * HEAD_DIM])

    @pl.when(j <= _SLOT_K)
    def _():
        wvec = jnp.where(j == _SLOT_Q, qnw_ref[...] * (HEAD_DIM ** -0.5 * LOG2E), knw_ref[...])

        def fn(h, r):
            y = _rms_rows(r) * wvec
            y = (y * c_ref[...] + pltpu.roll(y, HEAD_DIM - ROPE_HALF, 1) * s1_ref[...]
                 + pltpu.roll(y, ROPE_HALF, 1) * s2_ref[...])
            main_ref[0, 0, h] = y.astype(BF16)
        heads(fn)

    @pl.when((j == _SLOT_V) | (j == _SLOT_HI))
    def _():
        def fn(h, r):
            main_ref[0, 0, h] = r.astype(BF16)
        heads(fn)

    @pl.when(j == _SLOT_HQ)
    def _():
        def fn(h, r):
            main_ref[0, 0, h] = (_silu(r) * (HEAD_DIM ** -0.5)).astype(BF16)
        heads(fn)

    @pl.when(j == _SLOT_HG)
    def _():
        def fn(h, r):
            main_ref[0, 0, h] = _silu(r).astype(BF16)
        heads(fn)

    @pl.when(j == _N_GROUPS - 1)
    def _():
        def fn(h, r):
            gf_ref[0, h] = r
        heads(fn)


def _in_projection(x2, mod3, norm1_w, w_in_bf, qnw, knw, tabs, bsz, seq):
    t, d = x2.shape
    tm = 1024
    rpb = seq // tm

    def wcol(i, j):
        return (0, jnp.where(j < 4, j, jnp.where(j == _N_GROUPS - 1, 4, j + 1)))

    row = lambda i, j: (i, 0)
    const = lambda i, j: (0, 0)
    main_shape = jax.ShapeDtypeStruct((6, bsz, N_HEADS, seq, HEAD_DIM), BF16)
    gf_shape = jax.ShapeDtypeStruct((bsz, N_HEADS, seq, HEAD_DIM), F32)
    return pl.pallas_call(
        _inproj_kernel,
        grid=(t // tm, _N_GROUPS),
        in_specs=[pl.BlockSpec((tm, d), row),
                  pl.BlockSpec((1, 6, d), lambda i, j: (i // rpb, 0, 0)),
                  pl.BlockSpec((1, d), const),
                  pl.BlockSpec((d, GROUP_COLS), wcol),
                  pl.BlockSpec((1, HEAD_DIM), const),
                  pl.BlockSpec((1, HEAD_DIM), const),
                  pl.BlockSpec((tm, HEAD_DIM), row),
                  pl.BlockSpec((tm, HEAD_DIM), row),
                  pl.BlockSpec((tm, HEAD_DIM), row)],
        out_specs=[pl.BlockSpec((1, 1, N_HEADS, tm, HEAD_DIM),
                                lambda i, j: (jnp.minimum(j, 5), i // rpb, 0, i % rpb, 0)),
                   pl.BlockSpec((1, N_HEADS, tm, HEAD_DIM), lambda i, j: (i // rpb, 0, i % rpb, 0))],
        out_shape=[main_shape, gf_shape],
        scratch_shapes=[pltpu.VMEM((tm, d), BF16)],
        compiler_params=_params(("arbitrary", "arbitrary")),
        name="in_proj",
    )(x2, mod3, norm1_w.reshape(1, d), w_in_bf, qnw.reshape(1, HEAD_DIM), knw.reshape(1, HEAD_DIM), *tabs)


def _perm_matrix(dil):
    per = PERM_ROWS // dil
    r = lax.broadcasted_iota(jnp.int32, (PERM_ROWS, PERM_ROWS), 0)
    s = lax.broadcasted_iota(jnp.int32, (PERM_ROWS, PERM_ROWS), 1)
    src = (r & (per - 1)) * dil + lax.shift_right_logical(r, int(math.log2(per)))
    return (s == src).astype(BF16)


def _band_attention(q, kk, vv, bias):
    return _band_finish(_dot_nt(q, kk), vv, bias)


def _band_finish(s, vv, bias):
    s = s + bias
    m = jnp.max(s, axis=-1, keepdims=True)
    p = jnp.exp2(s - m)
    l = jnp.sum(p, axis=-1, keepdims=True)
    o = jnp.dot(p.astype(BF16), vv, preferred_element_type=F32)
    return o * (1.0 / l), m + jnp.log2(l)


def _run_pipelined(tasks):
    pending = tasks[0][0]()
    for i, (_, finish) in enumerate(tasks):
        nxt = tasks[i + 1][0]() if i + 1 < len(tasks) else None
        finish(pending)
        pending = nxt


def _attn_kernel(q_ref, k_ref, v_ref, o_ref, q4, k4, v4, q16, k16, v16, kp1, vp1, bias_scr, o_scr, l_scr):
    first = pl.program_id(2) == 0
    nblk = SPAN // ATTN_BLK
    ngrp = SPAN // PERM_ROWS

    @pl.when(first)
    def _():
        kp1[...] = jnp.zeros_like(kp1)
        vp1[...] = jnp.zeros_like(vp1)
        for buf in (k4, v4, k16, v16):
            buf[:, 0:ATTN_BLK, :] = jnp.zeros((buf.shape[0], ATTN_BLK, HEAD_DIM), BF16)

    qi = lax.broadcasted_iota(jnp.int32, (ATTN_BLK, 2 * ATTN_BLK), 0)
    kj = lax.broadcasted_iota(jnp.int32, (ATTN_BLK, 2 * ATTN_BLK), 1)
    dist = ATTN_BLK + qi - kj
    band = (dist >= 0) & (dist <= ATTN_BLK)
    bias_scr[0] = jnp.where(band, 0.0, -jnp.inf)
    first_key = jnp.where(first, ATTN_BLK, 0)
    bias_scr[1] = jnp.where(band & (kj >= first_key), 0.0, -jnp.inf)

    perm_tasks = []
    for dil, qd, kd, vd in ((DILATIONS[1], q4, k4, v4), (DILATIONS[2], q16, k16, v16)):
        perm = _perm_matrix(dil)
        per = PERM_ROWS // dil

        def scatter(dst, y, g, off, dil=dil, per=per):
            for c in range(dil):
                dst[c, off + g * per:off + (g + 1) * per, :] = y[c * per:(c + 1) * per, :]

        def rows(g):
            return slice(g * PERM_ROWS, (g + 1) * PERM_ROWS)

        def issue(a, ga, b, gb, perm=perm):
            x = jnp.concatenate([a[0, 0, 0, rows(ga), :], b[0, 0, 0, rows(gb), :]], axis=1)
            return jnp.dot(perm, x, preferred_element_type=F32)

        def finish(y, da, ga, db, gb, off, scatter=scatter):
            y = y.astype(BF16)
            scatter(da, y[:, :HEAD_DIM], ga, off)
            scatter(db, y[:, HEAD_DIM:], gb, off)

        for g in range(ngrp):
            perm_tasks.append((lambda g=g, issue=issue: issue(k_ref, g, v_ref, g),
                               lambda y, g=g, finish=finish, kd=kd, vd=vd: finish(y, kd, g, vd, g, ATTN_BLK)))
        for g in range(0, ngrp, 2):
            perm_tasks.append((lambda g=g, issue=issue: issue(q_ref, g, q_ref, g + 1),
                               lambda y, g=g, finish=finish, qd=qd: finish(y, qd, g, qd, g + 1, 0)))

    def attn_task(get_q, get_kk, get_vv, bias_idx, branch, start, stride):
        def finish(s):
            o, lse = _band_finish(s, get_vv(), bias_scr[bias_idx])
            rws = pl.ds(start, ATTN_BLK, stride=stride) if stride > 1 else pl.ds(start, ATTN_BLK)
            o_scr[branch, rws, :] = o
            l_scr[branch, rws, :] = jnp.broadcast_to(lse, (ATTN_BLK, HEAD_DIM))
        return (lambda: _dot_nt(get_q(), get_kk()), finish)

    nat_tasks = []
    for r in range(nblk):
        lo = r * ATTN_BLK
        if r == 0:
            get_kk = lambda: jnp.concatenate([kp1[...], k_ref[0, 0, 0, 0:ATTN_BLK, :]], axis=0)
            get_vv = lambda: jnp.concatenate([vp1[...], v_ref[0, 0, 0, 0:ATTN_BLK, :]], axis=0)
        else:
            get_kk = lambda lo=lo: k_ref[0, 0, 0, lo - ATTN_BLK:lo + ATTN_BLK, :]
            get_vv = lambda lo=lo: v_ref[0, 0, 0, lo - ATTN_BLK:lo + ATTN_BLK, :]
        nat_tasks.append(attn_task(lambda lo=lo: q_ref[0, 0, 0, lo:lo + ATTN_BLK, :], get_kk, get_vv,
                                   1 if r == 0 else 0, 0, lo, 1))

    cls_tasks = []
    dil = DILATIONS[1]
    for c in range(dil):
        for a in range(nblk // dil):
            lo = a * ATTN_BLK
            cls_tasks.append(attn_task(lambda c=c, lo=lo: q4[c, lo:lo + ATTN_BLK, :],
                                       lambda c=c, lo=lo: k4[c, lo:lo + 2 * ATTN_BLK, :],
                                       lambda c=c, lo=lo: v4[c, lo:lo + 2 * ATTN_BLK, :],
                                       1 if a == 0 else 0, 1, lo * dil + c, dil))
    dil = DILATIONS[2]
    for c in range(dil):
        cls_tasks.append(attn_task(lambda c=c: q16[c], lambda c=c: k16[c], lambda c=c: v16[c], 1, 2, c, dil))

    _run_pipelined(perm_tasks + nat_tasks + cls_tasks)

    kp1[...] = k_ref[0, 0, 0, SPAN - ATTN_BLK:SPAN, :]
    vp1[...] = v_ref[0, 0, 0, SPAN - ATTN_BLK:SPAN, :]
    for buf in (k4, v4, k16, v16):
        n = buf.shape[1]
        buf[:, 0:ATTN_BLK, :] = buf[:, n - ATTN_BLK:n, :]

    for r in range(nblk):
        rows = slice(r * ATTN_BLK, (r + 1) * ATTN_BLK)
        l0, l1, l2 = l_scr[0, rows, :], l_scr[1, rows, :], l_scr[2, rows, :]
        mx = jnp.maximum(jnp.maximum(l0, l1), l2)
        w0, w1, w2 = jnp.exp2(l0 - mx), jnp.exp2(l1 - mx), jnp.exp2(l2 - mx)
        acc = w0 * o_scr[0, rows, :] + w1 * o_scr[1, rows, :] + w2 * o_scr[2, rows, :]
        o_ref[0, rows, :] = (acc * (1.0 / (w0 + w1 + w2))).astype(o_ref.dtype)


def _dilated_attention(main, bsz, seq):
    nspan = seq // SPAN
    d4, d16 = DILATIONS[1], DILATIONS[2]
    blk = (1, 1, 1, SPAN, HEAD_DIM)
    spec = lambda slot: pl.BlockSpec(blk, lambda b, h, n, s=slot: (s, b, h, n, 0))
    cls = lambda dil, halo: pltpu.VMEM((dil, halo + SPAN // dil, HEAD_DIM), BF16)
    return pl.pallas_call(
        _attn_kernel,
        grid=(bsz, N_HEADS, nspan),
        in_specs=[spec(_SLOT_Q), spec(_SLOT_K), spec(_SLOT_V)],
        out_specs=pl.BlockSpec((1, SPAN, HEAD_DIM), lambda b, h, n: (b, n, h)),
        out_shape=jax.ShapeDtypeStruct((bsz, seq, GROUP_COLS), BF16),
        scratch_shapes=[cls(d4, 0), cls(d4, ATTN_BLK), cls(d4, ATTN_BLK),
                        cls(d16, 0), cls(d16, ATTN_BLK), cls(d16, ATTN_BLK),
                        pltpu.VMEM((ATTN_BLK, HEAD_DIM), BF16), pltpu.VMEM((ATTN_BLK, HEAD_DIM), BF16),
                        pltpu.VMEM((2, ATTN_BLK, 2 * ATTN_BLK), F32),
                        pltpu.VMEM((3, SPAN, HEAD_DIM), F32), pltpu.VMEM((3, SPAN, HEAD_DIM), F32)],
        compiler_params=_params(("arbitrary", "arbitrary", "arbitrary")),
        name="dilated_attn",
    )(main, main, main)


def _split3(x):
    hi = x.astype(BF16)
    r1 = x - hi.astype(F32)
    mid = r1.astype(BF16)
    lo = (r1 - mid.astype(F32)).astype(BF16)
    return hi, mid, lo


def _hgrn_chunks(qs, vs, gates, gs, lbs, nw, sts, consts):
    tril3, xor_ts, causal, lane_mod, levels = consts
    c_rows = HG_CHUNK
    heads = range(len(qs))

    bs, css = [], []
    for h in heads:
        sg = 1.0 / (1.0 + jnp.exp(-gs[h]))
        f = lbs[h] + (1.0 - lbs[h]) * sg
        l2k = jnp.log2(1.0 - f)
        hi, mid, lo = _split3(jnp.log2(f))
        b = jnp.dot(tril3, jnp.concatenate([hi, mid, lo], axis=0), preferred_element_type=F32)
        bs.append(b)
        css.append(b - l2k)
    b_lasts = [b[c_rows - 1:c_rows, :] for b in bs]

    os_ = [_dot_nt((qs[h] * jnp.exp2(bs[h])).astype(BF16), sts[h].astype(BF16)) for h in heads]

    a_mats = [None for _ in heads]
    for sz in levels:
        zero = jnp.zeros((sz, HEAD_DIM), F32)
        for h in heads:
            q, b, cs = qs[h], bs[h], css[h]
            qparts, kparts = [], []
            for p in range(c_rows // (2 * sz)):
                l0, r0, r1 = p * 2 * sz, p * 2 * sz + sz, (p + 1) * 2 * sz
                bref = b[r0 - 1:r0, :]
                kparts += [jnp.exp2(bref - cs[l0:r0, :]), zero]
                qparts += [zero, q[r0:r1, :] * jnp.exp2(b[r0:r1, :] - bref)]
            al = _dot_nt(jnp.concatenate(qparts, axis=0).astype(BF16),
                         jnp.concatenate(kparts, axis=0).astype(BF16))
            a_mats[h] = al if a_mats[h] is None else jnp.where(xor_ts < sz, a_mats[h], al)

    for h in heads:
        q, b, cs = qs[h], bs[h], css[h]
        blocks = []
        for i in range(c_rows // HG_SUB):
            rs = slice(i * HG_SUB, (i + 1) * HG_SUB)
            bb, qq, cc = b[rs, :], q[rs, :], cs[rs, :]
            ablk = jnp.zeros((HG_SUB, c_rows), F32)
            for s in range(HG_SUB):
                p = jnp.exp2(bb - jnp.broadcast_to(cc[s:s + 1, :], bb.shape)) * qq
                ablk = jnp.where(lane_mod == s, jnp.sum(p, axis=-1, keepdims=True), ablk)
            blocks.append(ablk)
        diag = jnp.where(causal, jnp.concatenate(blocks, axis=0), 0.0)
        a_mats[h] = jnp.where(xor_ts < HG_SUB, diag, a_mats[h]).astype(BF16)

    os_ = [os_[h] + jnp.dot(a_mats[h], vs[h], preferred_element_type=F32) for h in heads]

    st_new = []
    for h in heads:
        kdec = jnp.exp2(b_lasts[h] - css[h]).astype(BF16)
        upd = jnp.dot(vs[h].astype(F32).T.astype(BF16), kdec, preferred_element_type=F32)
        st_new.append(sts[h] * jnp.exp2(b_lasts[h]) + upd)

    ys = [(_rms_rows(os_[h]) * nw * gates[h].astype(F32)).astype(BF16) for h in heads]
    return list(zip(ys, st_new))


def _hgrn_kernel(q_ref, i_ref, g_ref, f_ref, lbl_ref, nw_ref, o_ref, st_scr):
    c_rows = HG_CHUNK
    nh = q_ref.shape[2]

    @pl.when(pl.program_id(2) == 0)
    def _():
        st_scr[...] = jnp.zeros_like(st_scr)

    lg = lbl_ref[...]
    e = jnp.exp(lg - jnp.max(lg, axis=0, keepdims=True))
    lb_all = e[0:1, :] / jnp.sum(e, axis=0, keepdims=True)

    ti = lax.broadcasted_iota(jnp.int32, (c_rows, c_rows), 0)
    si = lax.broadcasted_iota(jnp.int32, (c_rows, c_rows), 1)
    tril = (ti >= si).astype(BF16)
    levels = []
    sz = HG_SUB
    while sz < c_rows:
        levels.append(sz)
        sz *= 2
    lane_mod = lax.broadcasted_iota(jnp.int32, (HG_SUB, c_rows), 1) & (HG_SUB - 1)
    consts = (jnp.concatenate([tril, tril, tril], axis=1), ti ^ si, ti >= si, lane_mod, tuple(levels))

    def chunk(ci, carry):
        r0 = pl.multiple_of(ci * c_rows, c_rows)
        rows = pl.ds(r0, c_rows)
        hs = range(nh)
        outs = _hgrn_chunks([q_ref[0, 0, h, rows, :].astype(F32) for h in hs],
                            [i_ref[0, 0, h, rows, :] for h in hs],
                            [g_ref[0, 0, h, rows, :] for h in hs],
                            [f_ref[0, h, rows, :] for h in hs],
                            [lb_all[:, h * HEAD_DIM:(h + 1) * HEAD_DIM] for h in hs],
                            nw_ref[...], [st_scr[h] for h in hs], consts)
        for hh, (y, st_new) in enumerate(outs):
            st_scr[hh] = st_new
            o_ref[0, rows, hh * HEAD_DIM:(hh + 1) * HEAD_DIM] = y
        return carry

    lax.fori_loop(0, o_ref.shape[1] // c_rows, chunk, 0)


def _hgrn2(main, gf, lb_logits, hg_norm_w, bsz, seq):
    tt = 2048
    nh = HG_HEADS_PER_STEP
    blk = (1, 1, nh, tt, HEAD_DIM)
    spec = lambda slot: pl.BlockSpec(blk, lambda b, h, n, s=slot: (s, b, h, n, 0))
    return pl.pallas_call(
        _hgrn_kernel,
        grid=(bsz, N_HEADS // nh, seq // tt),
        in_specs=[spec(_SLOT_HQ), spec(_SLOT_HI), spec(_SLOT_HG),
                  pl.BlockSpec((1, nh, tt, HEAD_DIM), lambda b, h, n: (b, h, n, 0)),
                  pl.BlockSpec((lb_logits.shape[0], nh * HEAD_DIM), lambda b, h, n: (0, h)),
                  pl.BlockSpec((1, HEAD_DIM), lambda b, h, n: (0, 0))],
        out_specs=pl.BlockSpec((1, tt, nh * HEAD_DIM), lambda b, h, n: (b, n, h)),
        out_shape=jax.ShapeDtypeStruct((bsz, seq, GROUP_COLS), BF16),
        scratch_shapes=[pltpu.VMEM((nh, HEAD_DIM, HEAD_DIM), F32)],
        compiler_params=_params(("arbitrary", "arbitrary", "arbitrary")),
        name="hgrn2",
    )(main, main, main, gf, lb_logits, hg_norm_w.reshape(1, HEAD_DIM))


def _outproj_kernel(x_ref, a_ref, g_ref, mod_ref, aw_ref, w_ref, o_ref):
    half = a_ref.shape[1]
    an = (_rms_rows(a_ref[...].astype(F32)) * aw_ref[...]).astype(BF16)
    mix = (jnp.dot(an, w_ref[0:half, :], preferred_element_type=F32)
           + jnp.dot(g_ref[...], w_ref[half:, :], preferred_element_type=F32))
    o_ref[...] = x_ref[...] + mod_ref[0, 2:3, :] * mix


def _out_projection(x2, attn2, hg2, mod3, attn_norm_w, w_out_bf, seq):
    t, d = x2.shape
    half = attn2.shape[1]
    tm = 512
    rpb = seq // tm
    row = lambda i: (i, 0)
    return pl.pallas_call(
        _outproj_kernel,
        grid=(t // tm,),
        in_specs=[pl.BlockSpec((tm, d), row),
                  pl.BlockSpec((tm, half), row),
                  pl.BlockSpec((tm, half), row),
                  pl.BlockSpec((1, 6, d), lambda i: (i // rpb, 0, 0)),
                  pl.BlockSpec((1, half), lambda i: (0, 0)),
                  pl.BlockSpec((2 * half, d), lambda i: (0, 0))],
        out_specs=pl.BlockSpec((tm, d), row),
        out_shape=jax.ShapeDtypeStruct((t, d), F32),
        compiler_params=_params(("arbitrary",)),
        name="out_proj",
    )(x2, attn2, hg2, mod3, attn_norm_w.reshape(1, half), w_out_bf)


def _mlp_kernel(x_ref, mod_ref, nw_ref, w1_ref, w2_ref, o_ref, h_scr, acc_scr):
    k = pl.program_id(1)

    @pl.when(k == 0)
    def _():
        y = _rms_rows(x_ref[...]) * nw_ref[...]
        h_scr[...] = (y * (1.0 + mod_ref[0, 4:5, :]) + mod_ref[0, 3:4, :]).astype(BF16)
        acc_scr[...] = jnp.zeros_like(acc_scr)

    a = jnp.dot(h_scr[...], w1_ref[...], preferred_element_type=F32)
    a = jnp.square(jnp.maximum(a, 0.0)).astype(BF16)
    acc_scr[...] += jnp.dot(a, w2_ref[...], preferred_element_type=F32)

    @pl.when(k == pl.num_programs(1) - 1)
    def _():
        o_ref[...] = x_ref[...] + mod_ref[0, 5:6, :] * acc_scr[...]


def _mlp(x2, mod3, norm2_w, w1_bf, w2_bf, seq):
    t, d = x2.shape
    dff = w1_bf.shape[1]
    tm, tf = 512, 1024
    rpb = seq // tm
    row = lambda i, k: (i, 0)
    return pl.pallas_call(
        _mlp_kernel,
        grid=(t // tm, dff // tf),
        in_specs=[pl.BlockSpec((tm, d), row),
                  pl.BlockSpec((1, 6, d), lambda i, k: (i // rpb, 0, 0)),
                  pl.BlockSpec((1, d), lambda i, k: (0, 0)),
                  pl.BlockSpec((d, tf), lambda i, k: (0, k)),
                  pl.BlockSpec((tf, d), lambda i, k: (k, 0))],
        out_specs=pl.BlockSpec((tm, d), row),
        out_shape=jax.ShapeDtypeStruct((t, d), F32),
        scratch_shapes=[pltpu.VMEM((tm, d), BF16), pltpu.VMEM((tm, d), F32)],
        compiler_params=_params(("arbitrary", "arbitrary")),
        name="mlp",
    )(x2, mod3, norm2_w.reshape(1, d), w1_bf, w2_bf)


def kernel(x, c, positions, norm1_w, w_ada, b_ada, w_in, q_norm_w, k_norm_w, attn_out_norm_w,
           hg_lb_logits, hg_norm_w, w_out, norm2_w, w_ff1, w_ff2):
    bsz, seq, d = x.shape
    assert w_ada.shape[0] == 1, "single-layer block"
    assert seq % SPAN == 0 and d == 2 * GROUP_COLS and w_in.shape[2] == _N_GROUPS * GROUP_COLS
    x2 = x.reshape(bsz * seq, d)

    mod3 = _ada_mod(c, w_ada[0], b_ada[0]).reshape(bsz, 6, d)
    tabs = _rope_tables(positions)

    main, gf = _in_projection(x2, mod3, norm1_w[0], w_in[0].astype(BF16), q_norm_w[0], k_norm_w[0],
                              tabs, bsz, seq)
    attn = _dilated_attention(main, bsz, seq)
    hg = _hgrn2(main, gf, hg_lb_logits, hg_norm_w[0], bsz, seq)

    x1 = _out_projection(x2, attn.reshape(bsz * seq, GROUP_COLS), hg.reshape(bsz * seq, GROUP_COLS), mod3,
                         attn_out_norm_w[0], w_out[0].astype(BF16), seq)
    out = _mlp(x1, mod3, norm2_w[0], w_ff1[0].astype(BF16), w_ff2[0].astype(BF16), seq)
    return out.reshape(bsz, seq, d)
```

```python
import math

import jax
import jax.numpy as jnp
from jax import lax
from jax.experimental import pallas as pl
from jax.experimental.pallas import tpu as pltpu

F32 = jnp.float32
BF16 = jnp.bfloat16

HEAD_DIM = 128
N_HEADS = 8
GROUP_COLS = N_HEADS * HEAD_DIM
ROPE_THETA = 500000.0
ROPE_DIM = HEAD_DIM // 4
ROPE_HALF = ROPE_DIM // 2
DILATIONS = (1, 4, 16)
ATTN_BLK = 128
SPAN = ATTN_BLK * DILATIONS[-1]
PERM_ROWS = 256
HG_CHUNK = 128
HG_SUB = 8
HG_HEADS_PER_STEP = 4
EPS = 1e-6
LOG2E = math.log2(math.e)
VMEM_LIMIT = 56 * 1024 * 1024

_N_GROUPS = 7
_SLOT_Q, _SLOT_K, _SLOT_V, _SLOT_HQ, _SLOT_HI, _SLOT_HG = range(6)


def _params(sem, vmem=VMEM_LIMIT):
    return pltpu.CompilerParams(dimension_semantics=sem, vmem_limit_bytes=vmem)


def _silu(x):
    return x * (1.0 / (1.0 + jnp.exp(-x)))


def _rms_rows(x, eps=EPS):
    return x * lax.rsqrt(jnp.mean(x * x, axis=-1, keepdims=True) + eps)


def _dot_nt(a, b):
    return lax.dot_general(a, b, (((1,), (1,)), ((), ())), preferred_element_type=F32)


def _ada_kernel(c_ref, w_ref, b_ref, o_ref):
    sc = _silu(c_ref[...]).astype(BF16)
    o_ref[...] = jnp.dot(sc, w_ref[...].astype(BF16), preferred_element_type=F32) + b_ref[...]


def _ada_mod(c, w_ada, b_ada):
    bsz, d = c.shape
    n = w_ada.shape[1]
    tn = 1024
    rows = 8
    c8 = jnp.pad(c, ((0, rows - bsz), (0, 0)))
    out = pl.pallas_call(
        _ada_kernel,
        grid=(n // tn,),
        in_specs=[pl.BlockSpec((rows, d), lambda j: (0, 0)),
                  pl.BlockSpec((d, tn), lambda j: (0, j)),
                  pl.BlockSpec((1, tn), lambda j: (0, j))],
        out_specs=pl.BlockSpec((rows, tn), lambda j: (0, j)),
        out_shape=jax.ShapeDtypeStruct((rows, n), F32),
        compiler_params=_params(("arbitrary",)),
        name="ada_mod",
    )(c8, w_ada, b_ada.reshape(1, n))
    return out[:bsz]


def _rope_kernel(pos_ref, invf_ref, cos_ref, sin_ref):
    ang = pos_ref[...] * invf_ref[...]
    cos_ref[...] = jnp.cos(ang)
    sin_ref[...] = jnp.sin(ang)


def _rope_tables(positions):
    t = positions.size
    ts = 2048
    inv_freq = ROPE_THETA ** (-(jnp.arange(ROPE_HALF, dtype=F32) * 2.0) / ROPE_DIM)
    invf = jnp.broadcast_to(inv_freq[:, None], (ROPE_HALF, ts))
    pos = positions.astype(F32).reshape(1, t)
    tab = jax.ShapeDtypeStruct((ROPE_HALF, t), F32)
    spec = pl.BlockSpec((ROPE_HALF, ts), lambda i: (0, i))
    return pl.pallas_call(
        _rope_kernel,
        grid=(t // ts,),
        in_specs=[pl.BlockSpec((1, ts), lambda i: (0, i)),
                  pl.BlockSpec((ROPE_HALF, ts), lambda i: (0, 0))],
        out_specs=[spec, spec],
        out_shape=[tab, tab],
        compiler_params=_params(("arbitrary",)),
        name="rope_tables",
    )(pos, invf)


_PROJ_COLS = 2 * HEAD_DIM
_QK_TOKENS = 256


def _inproj_kernel(x_ref, mod_ref, n1w_ref, w_ref, wqk_ref, qg_ref, kg_ref, cos_ref, sin_ref,
                   main_ref, gf_ref, h_scr):
    j = pl.program_id(1)

    @pl.when(j == 0)
    def _():
        y = _rms_rows(x_ref[...]) * n1w_ref[...]
        shift = mod_ref[0, 0:1, :]
        scale = mod_ref[0, 1:2, :]
        h_scr[...] = (y * (1.0 + scale) + shift).astype(BF16)

    def heads(fn):
        for hp in range(GROUP_COLS // _PROJ_COLS):
            res = jnp.dot(h_scr[...], w_ref[:, hp * _PROJ_COLS:(hp + 1) * _PROJ_COLS],
                          preferred_element_type=F32)
            for hh in range(_PROJ_COLS // HEAD_DIM):
                fn(hp * (_PROJ_COLS // HEAD_DIM) + hh, res[:, hh * HEAD_DIM:(hh + 1) * HEAD_DIM])

    @pl.when(j <= _SLOT_K)
    def _():
        gain = jnp.where(j == _SLOT_Q, qg_ref[...], kg_ref[...])
        nt = _QK_TOKENS
        gain = jnp.concatenate([gain] * (nt // HEAD_DIM), axis=1)
        for part in range(h_scr.shape[0] // nt):
            toks = slice(part * nt, (part + 1) * nt)
            rt = _dot_nt(wqk_ref[...], h_scr[toks, :])
            cos, sin = cos_ref[:, toks], sin_ref[:, toks]
            for h in range(N_HEADS):
                r = rt[h * HEAD_DIM:(h + 1) * HEAD_DIM, :]
                y = r * lax.rsqrt(jnp.mean(r * r, axis=0, keepdims=True) + EPS) * gain
                x1, x2 = y[0:ROPE_HALF, :], y[ROPE_HALF:ROPE_DIM, :]
                y = jnp.concatenate([x1 * cos - x2 * sin, x2 * cos + x1 * sin, y[ROPE_DIM:, :]], axis=0)
                main_ref[0, 0, h, toks, :] = y.T.astype(BF16)

    @pl.when((j == _SLOT_V) | (j == _SLOT_HI))
    def _():
        def fn(h, r):
            main_ref[0, 0, h] = r.astype(BF16)
        heads(fn)

    @pl.when(j == _SLOT_HQ)
    def _():
        def fn(h, r):
            main_ref[0, 0, h] = (_silu(r) * (HEAD_DIM ** -0.5)).astype(BF16)
        heads(fn)

    @pl.when(j == _SLOT_HG)
    def _():
        def fn(h, r):
            main_ref[0, 0, h] = _silu(r).astype(BF16)
        heads(fn)

    @pl.when(j == _N_GROUPS - 1)
    def _():
        def fn(h, r):
            gf_ref[0, h] = r
        heads(fn)


def _in_projection(x2, mod3, norm1_w, w_in, qnw, knw, tabs, bsz, seq):
    t, d = x2.shape
    tm = 1024
    rpb = seq // tm
    wqk_t = w_in[:, :2 * GROUP_COLS].T.astype(BF16)
    w_rest = w_in[:, 2 * GROUP_COLS:].astype(BF16)
    lanes = lambda g: jnp.broadcast_to(g[:, None], (HEAD_DIM, HEAD_DIM))
    qg = lanes(qnw * (HEAD_DIM ** -0.5 * LOG2E))
    kg = lanes(knw)

    def wcol(i, j):
        return (0, jnp.where(j <= 2, 0, jnp.where(j == 3, 1, jnp.where(j == 4, 3, jnp.where(j == 5, 4, 2)))))

    row = lambda i, j: (i, 0)
    const = lambda i, j: (0, 0)
    main_shape = jax.ShapeDtypeStruct((6, bsz, N_HEADS, seq, HEAD_DIM), BF16)
    gf_shape = jax.ShapeDtypeStruct((bsz, N_HEADS, seq, HEAD_DIM), F32)
    return pl.pallas_call(
        _inproj_kernel,
        grid=(t // tm, _N_GROUPS),
        in_specs=[pl.BlockSpec((tm, d), row),
                  pl.BlockSpec((1, 6, d), lambda i, j: (i // rpb, 0, 0)),
                  pl.BlockSpec((1, d), const),
                  pl.BlockSpec((d, GROUP_COLS), wcol),
                  pl.BlockSpec((GROUP_COLS, d), lambda i, j: (jnp.minimum(j, 1), 0)),
                  pl.BlockSpec((HEAD_DIM, HEAD_DIM), const),
                  pl.BlockSpec((HEAD_DIM, HEAD_DIM), const),
                  pl.BlockSpec((ROPE_HALF, tm), lambda i, j: (0, i)),
                  pl.BlockSpec((ROPE_HALF, tm), lambda i, j: (0, i))],
        out_specs=[pl.BlockSpec((1, 1, N_HEADS, tm, HEAD_DIM),
                                lambda i, j: (jnp.minimum(j, 5), i // rpb, 0, i % rpb, 0)),
                   pl.BlockSpec((1, N_HEADS, tm, HEAD_DIM), lambda i, j: (i // rpb, 0, i % rpb, 0))],
        out_shape=[main_shape, gf_shape],
        scratch_shapes=[pltpu.VMEM((tm, d), BF16)],
        compiler_params=_params(("arbitrary", "arbitrary"), vmem=58 * 1024 * 1024),
        name="in_proj",
    )(x2, mod3, norm1_w.reshape(1, d), w_rest, wqk_t, qg, kg, *tabs)


def _perm_matrix(dil):
    per = PERM_ROWS // dil
    r = lax.broadcasted_iota(jnp.int32, (PERM_ROWS, PERM_ROWS), 0)
    s = lax.broadcasted_iota(jnp.int32, (PERM_ROWS, PERM_ROWS), 1)
    src = (r & (per - 1)) * dil + lax.shift_right_logical(r, int(math.log2(per)))
    return (s == src).astype(BF16)


def _band_attention(q, kk, vv, bias):
    return _band_finish(_dot_nt(q, kk), vv, bias)


def _band_finish(s, vv, bias):
    s = s + bias
    m = jnp.max(s, axis=-1, keepdims=True)
    p = jnp.exp2(s - m)
    l = jnp.sum(p, axis=-1, keepdims=True)
    o = jnp.dot(p.astype(BF16), vv, preferred_element_type=F32)
    return o * (1.0 / l), m + jnp.log2(l)


def _run_pipelined(tasks):
    pending = tasks[0][0]()
    for i, (_, finish) in enumerate(tasks):
        nxt = tasks[i + 1][0]() if i + 1 < len(tasks) else None
        finish(pending)
        pending = nxt


def _attn_kernel(q_ref, k_ref, v_ref, o_ref, q4, k4, v4, q16, k16, v16, kp1, vp1, bias_scr, o_scr, l_scr):
    first = pl.program_id(2) == 0
    nblk = SPAN // ATTN_BLK
    ngrp = SPAN // PERM_ROWS

    @pl.when(first)
    def _():
        kp1[...] = jnp.zeros_like(kp1)
        vp1[...] = jnp.zeros_like(vp1)
        for buf in (k4, v4, k16, v16):
            buf[:, 0:ATTN_BLK, :] = jnp.zeros((buf.shape[0], ATTN_BLK, HEAD_DIM), BF16)

    qi = lax.broadcasted_iota(jnp.int32, (ATTN_BLK, 2 * ATTN_BLK), 0)
    kj = lax.broadcasted_iota(jnp.int32, (ATTN_BLK, 2 * ATTN_BLK), 1)
    dist = ATTN_BLK + qi - kj
    band = (dist >= 0) & (dist <= ATTN_BLK)
    bias_scr[0] = jnp.where(band, 0.0, -jnp.inf)
    first_key = jnp.where(first, ATTN_BLK, 0)
    bias_scr[1] = jnp.where(band & (kj >= first_key), 0.0, -jnp.inf)

    perm_tasks = []
    for dil, qd, kd, vd in ((DILATIONS[1], q4, k4, v4), (DILATIONS[2], q16, k16, v16)):
        perm = _perm_matrix(dil)
        per = PERM_ROWS // dil

        def scatter(dst, y, g, off, dil=dil, per=per):
            for c in range(dil):
                dst[c, off + g * per:off + (g + 1) * per, :] = y[c * per:(c + 1) * per, :]

        def rows(g):
            return slice(g * PERM_ROWS, (g + 1) * PERM_ROWS)

        def issue(a, ga, b, gb, perm=perm):
            x = jnp.concatenate([a[0, 0, 0, rows(ga), :], b[0, 0, 0, rows(gb), :]], axis=1)
            return jnp.dot(perm, x, preferred_element_type=F32)

        def finish(y, da, ga, db, gb, off, scatter=scatter):
            y = y.astype(BF16)
            scatter(da, y[:, :HEAD_DIM], ga, off)
            scatter(db, y[:, HEAD_DIM:], gb, off)

        for g in range(ngrp):
            perm_tasks.append((lambda g=g, issue=issue: issue(k_ref, g, v_ref, g),
                               lambda y, g=g, finish=finish, kd=kd, vd=vd: finish(y, kd, g, vd, g, ATTN_BLK)))
        for g in range(0, ngrp, 2):
            perm_tasks.append((lambda g=g, issue=issue: issue(q_ref, g, q_ref, g + 1),
                               lambda y, g=g, finish=finish, qd=qd: finish(y, qd, g, qd, g + 1, 0)))

    def attn_task(get_q, get_kk, get_vv, bias_idx, branch, start, stride):
        def finish(s):
            o, lse = _band_finish(s, get_vv(), bias_scr[bias_idx])
            rws = pl.ds(start, ATTN_BLK, stride=stride) if stride > 1 else pl.ds(start, ATTN_BLK)
            o_scr[branch, rws, :] = o
            l_scr[branch, rws, :] = jnp.broadcast_to(lse, (ATTN_BLK, HEAD_DIM))
        return (lambda: _dot_nt(get_q(), get_kk()), finish)

    nat_tasks = []
    for r in range(nblk):
        lo = r * ATTN_BLK
        if r == 0:
            get_kk = lambda: jnp.concatenate([kp1[...], k_ref[0, 0, 0, 0:ATTN_BLK, :]], axis=0)
            get_vv = lambda: jnp.concatenate([vp1[...], v_ref[0, 0, 0, 0:ATTN_BLK, :]], axis=0)
        else:
            get_kk = lambda lo=lo: k_ref[0, 0, 0, lo - ATTN_BLK:lo + ATTN_BLK, :]
            get_vv = lambda lo=lo: v_ref[0, 0, 0, lo - ATTN_BLK:lo + ATTN_BLK, :]
        nat_tasks.append(attn_task(lambda lo=lo: q_ref[0, 0, 0, lo:lo + ATTN_BLK, :], get_kk, get_vv,
                                   1 if r == 0 else 0, 0, lo, 1))

    cls_tasks = []
    dil = DILATIONS[1]
    for c in range(dil):
        for a in range(nblk // dil):
            lo = a * ATTN_BLK
            cls_tasks.append(attn_task(lambda c=c, lo=lo: q4[c, lo:lo + ATTN_BLK, :],
                                       lambda c=c, lo=lo: k4[c, lo:lo + 2 * ATTN_BLK, :],
                                       lambda c=c, lo=lo: v4[c, lo:lo + 2 * ATTN_BLK, :],
                                       1 if a == 0 else 0, 1, lo * dil + c, dil))
    dil = DILATIONS[2]
    for c in range(dil):
        cls_tasks.append(attn_task(lambda c=c: q16[c], lambda c=c: k16[c], lambda c=c: v16[c], 1, 2, c, dil))

    _run_pipelined(perm_tasks + nat_tasks + cls_tasks)

    kp1[...] = k_ref[0, 0, 0, SPAN - ATTN_BLK:SPAN, :]
    vp1[...] = v_ref[0, 0, 0, SPAN - ATTN_BLK:SPAN, :]
    for buf in (k4, v4, k16, v16):
        n = buf.shape[1]
        buf[:, 0:ATTN_BLK, :] = buf[:, n - ATTN_BLK:n, :]

    for r in range(nblk):
        rows = slice(r * ATTN_BLK, (r + 1) * ATTN_BLK)
        l0, l1, l2 = l_scr[0, rows, :], l_scr[1, rows, :], l_scr[2, rows, :]
        mx = jnp.maximum(jnp.maximum(l0, l1), l2)
        w0, w1, w2 = jnp.exp2(l0 - mx), jnp.exp2(l1 - mx), jnp.exp2(l2 - mx)
        acc = w0 * o_scr[0, rows, :] + w1 * o_scr[1, rows, :] + w2 * o_scr[2, rows, :]
        o_ref[0, rows, :] = (acc * (1.0 / (w0 + w1 + w2))).astype(o_ref.dtype)


def _dilated_attention(main, bsz, seq):
    nspan = seq // SPAN
    d4, d16 = DILATIONS[1], DILATIONS[2]
    blk = (1, 1, 1, SPAN, HEAD_DIM)
    spec = lambda slot: pl.BlockSpec(blk, lambda b, h, n, s=slot: (s, b, h, n, 0))
    cls = lambda dil, halo: pltpu.VMEM((dil, halo + SPAN // dil, HEAD_DIM), BF16)
    return pl.pallas_call(
        _attn_kernel,
        grid=(bsz, N_HEADS, nspan),
        in_specs=[spec(_SLOT_Q), spec(_SLOT_K), spec(_SLOT_V)],
        out_specs=pl.BlockSpec((1, SPAN, HEAD_DIM), lambda b, h, n: (b, n, h)),
        out_shape=jax.ShapeDtypeStruct((bsz, seq, GROUP_COLS), BF16),
        scratch_shapes=[cls(d4, 0), cls(d4, ATTN_BLK), cls(d4, ATTN_BLK),
                        cls(d16, 0), cls(d16, ATTN_BLK), cls(d16, ATTN_BLK),
                        pltpu.VMEM((ATTN_BLK, HEAD_DIM), BF16), pltpu.VMEM((ATTN_BLK, HEAD_DIM), BF16),
                        pltpu.VMEM((2, ATTN_BLK, 2 * ATTN_BLK), F32),
                        pltpu.VMEM((3, SPAN, HEAD_DIM), F32), pltpu.VMEM((3, SPAN, HEAD_DIM), F32)],
        compiler_params=_params(("arbitrary", "arbitrary", "arbitrary")),
        name="dilated_attn",
    )(main, main, main)


def _split3(x):
    hi = x.astype(BF16)
    r1 = x - hi.astype(F32)
    mid = r1.astype(BF16)
    lo = (r1 - mid.astype(F32)).astype(BF16)
    return hi, mid, lo


def _hgrn_chunks(qs, vs, gates, gs, lbs, nw, sts, consts):
    tril3, xor_ts, causal, lane_mod, levels = consts
    c_rows = HG_CHUNK
    heads = range(len(qs))

    bs, css = [], []
    for h in heads:
        sg = 1.0 / (1.0 + jnp.exp(-gs[h]))
        f = lbs[h] + (1.0 - lbs[h]) * sg
        l2k = jnp.log2(1.0 - f)
        hi, mid, lo = _split3(jnp.log2(f))
        b = jnp.dot(tril3, jnp.concatenate([hi, mid, lo], axis=0), preferred_element_type=F32)
        bs.append(b)
        css.append(b - l2k)
    b_lasts = [b[c_rows - 1:c_rows, :] for b in bs]

    os_ = [_dot_nt((qs[h] * jnp.exp2(bs[h])).astype(BF16), sts[h].astype(BF16)) for h in heads]

    a_mats = [None for _ in heads]
    for sz in levels:
        zero = jnp.zeros((sz, HEAD_DIM), F32)
        for h in heads:
            q, b, cs = qs[h], bs[h], css[h]
            qparts, kparts = [], []
            for p in range(c_rows // (2 * sz)):
                l0, r0, r1 = p * 2 * sz, p * 2 * sz + sz, (p + 1) * 2 * sz
                bref = b[r0 - 1:r0, :]
                kparts += [jnp.exp2(bref - cs[l0:r0, :]), zero]
                qparts += [zero, q[r0:r1, :] * jnp.exp2(b[r0:r1, :] - bref)]
            al = _dot_nt(jnp.concatenate(qparts, axis=0).astype(BF16),
                         jnp.concatenate(kparts, axis=0).astype(BF16))
            a_mats[h] = al if a_mats[h] is None else jnp.where(xor_ts < sz, a_mats[h], al)

    for h in heads:
        q, b, cs = qs[h], bs[h], css[h]
        blocks = []
        for i in range(c_rows // HG_SUB):
            rs = slice(i * HG_SUB, (i + 1) * HG_SUB)
            bb, qq, cc = b[rs, :], q[rs, :], cs[rs, :]
            ablk = jnp.zeros((HG_SUB, c_rows), F32)
            for s in range(HG_SUB):
                p = jnp.exp2(bb - jnp.broadcast_to(cc[s:s + 1, :], bb.shape)) * qq
                ablk = jnp.where(lane_mod[s], jnp.sum(p, axis=-1, keepdims=True), ablk)
            blocks.append(ablk)
        diag = jnp.where(causal, jnp.concatenate(blocks, axis=0), 0.0)
        a_mats[h] = jnp.where(xor_ts < HG_SUB, diag, a_mats[h]).astype(BF16)

    os_ = [os_[h] + jnp.dot(a_mats[h], vs[h], preferred_element_type=F32) for h in heads]

    st_new = []
    for h in heads:
        kdec = jnp.exp2(b_lasts[h] - css[h]).astype(BF16)
        upd = jnp.dot(vs[h].astype(F32).T.astype(BF16), kdec, preferred_element_type=F32)
        st_new.append(sts[h] * jnp.exp2(b_lasts[h]) + upd)

    ys = [(_rms_rows(os_[h]) * nw * gates[h].astype(F32)).astype(BF16) for h in heads]
    return list(zip(ys, st_new))


def _hgrn_kernel(q_ref, i_ref, g_ref, f_ref, lbl_ref, nw_ref, o_ref, st_scr):
    c_rows = HG_CHUNK
    nh = q_ref.shape[2]

    @pl.when(pl.program_id(2) == 0)
    def _():
        st_scr[...] = jnp.zeros_like(st_scr)

    lg = lbl_ref[...]
    e = jnp.exp(lg - jnp.max(lg, axis=0, keepdims=True))
    lb_all = e[0:1, :] / jnp.sum(e, axis=0, keepdims=True)

    ti = lax.broadcasted_iota(jnp.int32, (c_rows, c_rows), 0)
    si = lax.broadcasted_iota(jnp.int32, (c_rows, c_rows), 1)
    tril = (ti >= si).astype(BF16)
    levels = []
    sz = HG_SUB
    while sz < c_rows:
        levels.append(sz)
        sz *= 2
    lane_in_sub = lax.broadcasted_iota(jnp.int32, (HG_SUB, c_rows), 1) & (HG_SUB - 1)
    lane_mod = tuple(lane_in_sub == s for s in range(HG_SUB))
    consts = (jnp.concatenate([tril, tril, tril], axis=1), ti ^ si, ti >= si, lane_mod, tuple(levels))

    def chunk(ci, carry):
        r0 = pl.multiple_of(ci * c_rows, c_rows)
        rows = pl.ds(r0, c_rows)
        hs = range(nh)
        outs = _hgrn_chunks([q_ref[0, 0, h, rows, :].astype(F32) for h in hs],
                            [i_ref[0, 0, h, rows, :] for h in hs],
                            [g_ref[0, 0, h, rows, :] for h in hs],
                            [f_ref[0, h, rows, :] for h in hs],
                            [lb_all[:, h * HEAD_DIM:(h + 1) * HEAD_DIM] for h in hs],
                            nw_ref[...], [st_scr[h] for h in hs], consts)
        for hh, (y, st_new) in enumerate(outs):
            st_scr[hh] = st_new
            o_ref[0, rows, hh * HEAD_DIM:(hh + 1) * HEAD_DIM] = y
        return carry

    lax.fori_loop(0, o_ref.shape[1] // c_rows, chunk, 0)


def _hgrn2(main, gf, lb_logits, hg_norm_w, bsz, seq):
    tt = 2048
    nh = HG_HEADS_PER_STEP
    blk = (1, 1, nh, tt, HEAD_DIM)
    spec = lambda slot: pl.BlockSpec(blk, lambda b, h, n, s=slot: (s, b, h, n, 0))
    return pl.pallas_call(
        _hgrn_kernel,
        grid=(bsz, N_HEADS // nh, seq // tt),
        in_specs=[spec(_SLOT_HQ), spec(_SLOT_HI), spec(_SLOT_HG),
                  pl.BlockSpec((1, nh, tt, HEAD_DIM), lambda b, h, n: (b, h, n, 0)),
                  pl.BlockSpec((lb_logits.shape[0], nh * HEAD_DIM), lambda b, h, n: (0, h)),
                  pl.BlockSpec((1, HEAD_DIM), lambda b, h, n: (0, 0))],
        out_specs=pl.BlockSpec((1, tt, nh * HEAD_DIM), lambda b, h, n: (b, n, h)),
        out_shape=jax.ShapeDtypeStruct((bsz, seq, GROUP_COLS), BF16),
        scratch_shapes=[pltpu.VMEM((nh, HEAD_DIM, HEAD_DIM), F32)],
        compiler_params=_params(("arbitrary", "arbitrary", "arbitrary")),
        name="hgrn2",
    )(main, main, main, gf, lb_logits, hg_norm_w.reshape(1, HEAD_DIM))


def _outproj_kernel(x_ref, a_ref, g_ref, mod_ref, aw_ref, w_ref, o_ref):
    half = a_ref.shape[1]
    an = (_rms_rows(a_ref[...].astype(F32)) * aw_ref[...]).astype(BF16)
    mix = (jnp.dot(an, w_ref[0:half, :], preferred_element_type=F32)
           + jnp.dot(g_ref[...], w_ref[half:, :], preferred_element_type=F32))
    o_ref[...] = x_ref[...] + mod_ref[0, 2:3, :] * mix


def _out_projection(x2, attn2, hg2, mod3, attn_norm_w, w_out_bf, seq):
    t, d = x2.shape
    half = attn2.shape[1]
    tm = 512
    rpb = seq // tm
    row = lambda i: (i, 0)
    return pl.pallas_call(
        _outproj_kernel,
        grid=(t // tm,),
        in_specs=[pl.BlockSpec((tm, d), row),
                  pl.BlockSpec((tm, half), row),
                  pl.BlockSpec((tm, half), row),
                  pl.BlockSpec((1, 6, d), lambda i: (i // rpb, 0, 0)),
                  pl.BlockSpec((1, half), lambda i: (0, 0)),
                  pl.BlockSpec((2 * half, d), lambda i: (0, 0))],
        out_specs=pl.BlockSpec((tm, d), row),
        out_shape=jax.ShapeDtypeStruct((t, d), F32),
        compiler_params=_params(("arbitrary",)),
        name="out_proj",
    )(x2, attn2, hg2, mod3, attn_norm_w.reshape(1, half), w_out_bf)


def _mlp_kernel(x_ref, mod_ref, nw_ref, w1_ref, w2_ref, o_ref, h_scr, acc_scr):
    k = pl.program_id(1)

    @pl.when(k == 0)
    def _():
        y = _rms_rows(x_ref[...]) * nw_ref[...]
        h_scr[...] = (y * (1.0 + mod_ref[0, 4:5, :]) + mod_ref[0, 3:4, :]).astype(BF16)
        acc_scr[...] = jnp.zeros_like(acc_scr)

    a = jnp.dot(h_scr[...], w1_ref[...], preferred_element_type=F32)
    a = jnp.square(jnp.maximum(a, 0.0)).astype(BF16)
    acc_scr[...] += jnp.dot(a, w2_ref[...], preferred_element_type=F32)

    @pl.when(k == pl.num_programs(1) - 1)
    def _():
        o_ref[...] = x_ref[...] + mod_ref[0, 5:6, :] * acc_scr[...]


def _mlp(x2, mod3, norm2_w, w1_bf, w2_bf, seq):
    t, d = x2.shape
    dff = w1_bf.shape[1]
    tm, tf = 512, 1024
    rpb = seq // tm
    row = lambda i, k: (i, 0)
    return pl.pallas_call(
        _mlp_kernel,
        grid=(t // tm, dff // tf),
        in_specs=[pl.BlockSpec((tm, d), row),
                  pl.BlockSpec((1, 6, d), lambda i, k: (i // rpb, 0, 0)),
                  pl.BlockSpec((1, d), lambda i, k: (0, 0)),
                  pl.BlockSpec((d, tf), lambda i, k: (0, k)),
                  pl.BlockSpec((tf, d), lambda i, k: (k, 0))],
        out_specs=pl.BlockSpec((tm, d), row),
        out_shape=jax.ShapeDtypeStruct((t, d), F32),
        scratch_shapes=[pltpu.VMEM((tm, d), BF16), pltpu.VMEM((tm, d), F32)],
        compiler_params=_params(("arbitrary", "arbitrary")),
        name="mlp",
    )(x2, mod3, norm2_w.reshape(1, d), w1_bf, w2_bf)


def kernel(x, c, positions, norm1_w, w_ada, b_ada, w_in, q_norm_w, k_norm_w, attn_out_norm_w,
           hg_lb_logits, hg_norm_w, w_out, norm2_w, w_ff1, w_ff2):
    bsz, seq, d = x.shape
    assert w_ada.shape[0] == 1, "single-layer block"
    assert seq % SPAN == 0 and d == 2 * GROUP_COLS and w_in.shape[2] == _N_GROUPS * GROUP_COLS
    x2 = x.reshape(bsz * seq, d)

    mod3 = _ada_mod(c, w_ada[0], b_ada[0]).reshape(bsz, 6, d)
    tabs = _rope_tables(positions)

    main, gf = _in_projection(x2, mod3, norm1_w[0], w_in[0], q_norm_w[0], k_norm_w[0], tabs, bsz, seq)
    attn = _dilated_attention(main, bsz, seq)
    hg = _hgrn2(main, gf, hg_lb_logits, hg_norm_w[0], bsz, seq)

    x1 = _out_projection(x2, attn.reshape(bsz * seq, GROUP_COLS), hg.reshape(bsz * seq, GROUP_COLS), mod3,
                         attn_out_norm_w[0], w_out[0].astype(BF16), seq)
    out = _mlp(x1, mod3, norm2_w[0], w_ff1[0].astype(BF16), w_ff2[0].astype(BF16), seq)
    return out.reshape(bsz, seq, d)
```

```python
import math

import jax
import jax.numpy as jnp
from jax import lax
from jax.experimental import pallas as pl
from jax.experimental.pallas import tpu as pltpu

F32 = jnp.float32
BF16 = jnp.bfloat16

HEAD_DIM = 128
N_HEADS = 8
GROUP_COLS = N_HEADS * HEAD_DIM
ROPE_THETA = 500000.0
ROPE_DIM = HEAD_DIM // 4
ROPE_HALF = ROPE_DIM // 2
DILATIONS = (1, 4, 16)
ATTN_BLK = 128
SPAN = ATTN_BLK * DILATIONS[-1]
PERM_ROWS = 256
HG_CHUNK = 128
HG_SUB = 8
HG_HEADS_PER_STEP = 8
EPS = 1e-6
LOG2E = math.log2(math.e)
VMEM_LIMIT = 56 * 1024 * 1024

_N_GROUPS = 7
_SLOT_Q, _SLOT_K, _SLOT_V, _SLOT_HQ, _SLOT_HI, _SLOT_HG = range(6)


def _params(sem, vmem=VMEM_LIMIT):
    return pltpu.CompilerParams(dimension_semantics=sem, vmem_limit_bytes=vmem)


def _silu(x):
    return x * (1.0 / (1.0 + jnp.exp(-x)))


def _rms_rows(x, eps=EPS):
    return x * lax.rsqrt(jnp.mean(x * x, axis=-1, keepdims=True) + eps)


def _dot_nt(a, b):
    return lax.dot_general(a, b, (((1,), (1,)), ((), ())), preferred_element_type=F32)


def _ada_kernel(c_ref, w_ref, b_ref, o_ref):
    sc = _silu(c_ref[...]).astype(BF16)
    o_ref[...] = jnp.dot(sc, w_ref[...].astype(BF16), preferred_element_type=F32) + b_ref[...]


def _ada_mod(c, w_ada, b_ada):
    bsz, d = c.shape
    n = w_ada.shape[1]
    tn = 1024
    rows = 8
    c8 = jnp.pad(c, ((0, rows - bsz), (0, 0)))
    out = pl.pallas_call(
        _ada_kernel,
        grid=(n // tn,),
        in_specs=[pl.BlockSpec((rows, d), lambda j: (0, 0)),
                  pl.BlockSpec((d, tn), lambda j: (0, j)),
                  pl.BlockSpec((1, tn), lambda j: (0, j))],
        out_specs=pl.BlockSpec((rows, tn), lambda j: (0, j)),
        out_shape=jax.ShapeDtypeStruct((rows, n), F32),
        compiler_params=_params(("arbitrary",)),
        name="ada_mod",
    )(c8, w_ada, b_ada.reshape(1, n))
    return out[:bsz]


def _rope_kernel(pos_ref, invf_ref, cos_ref, sin_ref):
    ang = pos_ref[...] * invf_ref[...]
    cos_ref[...] = jnp.cos(ang)
    sin_ref[...] = jnp.sin(ang)


def _rope_tables(positions):
    t = positions.size
    ts = 2048
    inv_freq = ROPE_THETA ** (-(jnp.arange(ROPE_HALF, dtype=F32) * 2.0) / ROPE_DIM)
    invf = jnp.broadcast_to(inv_freq[:, None], (ROPE_HALF, ts))
    pos = positions.astype(F32).reshape(1, t)
    tab = jax.ShapeDtypeStruct((ROPE_HALF, t), F32)
    spec = pl.BlockSpec((ROPE_HALF, ts), lambda i: (0, i))
    return pl.pallas_call(
        _rope_kernel,
        grid=(t // ts,),
        in_specs=[pl.BlockSpec((1, ts), lambda i: (0, i)),
                  pl.BlockSpec((ROPE_HALF, ts), lambda i: (0, 0))],
        out_specs=[spec, spec],
        out_shape=[tab, tab],
        compiler_params=_params(("arbitrary",)),
        name="rope_tables",
    )(pos, invf)


_PROJ_COLS = 2 * HEAD_DIM
_QK_TOKENS = 256


def _inproj_kernel(x_ref, mod_ref, n1w_ref, w_ref, wqk_ref, qg_ref, kg_ref, cos_ref, sin_ref,
                   main_ref, gf_ref, h_scr):
    j = pl.program_id(1)

    @pl.when(j == 0)
    def _():
        y = _rms_rows(x_ref[...]) * n1w_ref[...]
        shift = mod_ref[0, 0:1, :]
        scale = mod_ref[0, 1:2, :]
        h_scr[...] = (y * (1.0 + scale) + shift).astype(BF16)

    def heads(fn):
        for hp in range(GROUP_COLS // _PROJ_COLS):
            res = jnp.dot(h_scr[...], w_ref[:, hp * _PROJ_COLS:(hp + 1) * _PROJ_COLS],
                          preferred_element_type=F32)
            for hh in range(_PROJ_COLS // HEAD_DIM):
                fn(hp * (_PROJ_COLS // HEAD_DIM) + hh, res[:, hh * HEAD_DIM:(hh + 1) * HEAD_DIM])

    @pl.when(j <= _SLOT_K)
    def _():
        gain = jnp.where(j == _SLOT_Q, qg_ref[...], kg_ref[...])
        nt = _QK_TOKENS
        gain = jnp.concatenate([gain] * (nt // HEAD_DIM), axis=1)
        for part in range(h_scr.shape[0] // nt):
            toks = slice(part * nt, (part + 1) * nt)
            rt = _dot_nt(wqk_ref[...], h_scr[toks, :])
            cos, sin = cos_ref[:, toks], sin_ref[:, toks]
            for h in range(N_HEADS):
                r = rt[h * HEAD_DIM:(h + 1) * HEAD_DIM, :]
                y = r * lax.rsqrt(jnp.mean(r * r, axis=0, keepdims=True) + EPS) * gain
                x1, x2 = y[0:ROPE_HALF, :], y[ROPE_HALF:ROPE_DIM, :]
                y = jnp.concatenate([x1 * cos - x2 * sin, x2 * cos + x1 * sin, y[ROPE_DIM:, :]], axis=0)
                main_ref[0, 0, h, toks, :] = y.T.astype(BF16)

    @pl.when((j == _SLOT_V) | (j == _SLOT_HI))
    def _():
        def fn(h, r):
            main_ref[0, 0, h] = r.astype(BF16)
        heads(fn)

    @pl.when(j == _SLOT_HQ)
    def _():
        def fn(h, r):
            main_ref[0, 0, h] = (_silu(r) * (HEAD_DIM ** -0.5)).astype(BF16)
        heads(fn)

    @pl.when(j == _SLOT_HG)
    def _():
        def fn(h, r):
            main_ref[0, 0, h] = _silu(r).astype(BF16)
        heads(fn)

    @pl.when(j == _N_GROUPS - 1)
    def _():
        def fn(h, r):
            gf_ref[0, h] = r
        heads(fn)


def _in_projection(x2, mod3, norm1_w, w_in, qnw, knw, tabs, bsz, seq):
    t, d = x2.shape
    tm = 1024
    rpb = seq // tm
    w_bf = w_in.astype(BF16)
    wqk_t = w_bf[:, :2 * GROUP_COLS].T
    lanes = lambda g: jnp.broadcast_to(g[:, None], (HEAD_DIM, HEAD_DIM))
    qg = lanes(qnw * (HEAD_DIM ** -0.5 * LOG2E))
    kg = lanes(knw)

    def wcol(i, j):
        return (0, jnp.where(j <= 2, 2, jnp.where(j == 3, 3, jnp.where(j == 4, 5, jnp.where(j == 5, 6, 4)))))

    row = lambda i, j: (i, 0)
    const = lambda i, j: (0, 0)
    main_shape = jax.ShapeDtypeStruct((6, bsz, N_HEADS, seq, HEAD_DIM), BF16)
    gf_shape = jax.ShapeDtypeStruct((bsz, N_HEADS, seq, HEAD_DIM), F32)
    return pl.pallas_call(
        _inproj_kernel,
        grid=(t // tm, _N_GROUPS),
        in_specs=[pl.BlockSpec((tm, d), row),
                  pl.BlockSpec((1, 6, d), lambda i, j: (i // rpb, 0, 0)),
                  pl.BlockSpec((1, d), const),
                  pl.BlockSpec((d, GROUP_COLS), wcol),
                  pl.BlockSpec((GROUP_COLS, d), lambda i, j: (jnp.minimum(j, 1), 0)),
                  pl.BlockSpec((HEAD_DIM, HEAD_DIM), const),
                  pl.BlockSpec((HEAD_DIM, HEAD_DIM), const),
                  pl.BlockSpec((ROPE_HALF, tm), lambda i, j: (0, i)),
                  pl.BlockSpec((ROPE_HALF, tm), lambda i, j: (0, i))],
        out_specs=[pl.BlockSpec((1, 1, N_HEADS, tm, HEAD_DIM),
                                lambda i, j: (jnp.minimum(j, 5), i // rpb, 0, i % rpb, 0)),
                   pl.BlockSpec((1, N_HEADS, tm, HEAD_DIM), lambda i, j: (i // rpb, 0, i % rpb, 0))],
        out_shape=[main_shape, gf_shape],
        scratch_shapes=[pltpu.VMEM((tm, d), BF16)],
        compiler_params=_params(("arbitrary", "arbitrary"), vmem=58 * 1024 * 1024),
        name="in_proj",
    )(x2, mod3, norm1_w.reshape(1, d), w_bf, wqk_t, qg, kg, *tabs)


def _perm_matrix(dil):
    per = PERM_ROWS // dil
    r = lax.broadcasted_iota(jnp.int32, (PERM_ROWS, PERM_ROWS), 0)
    s = lax.broadcasted_iota(jnp.int32, (PERM_ROWS, PERM_ROWS), 1)
    src = (r & (per - 1)) * dil + lax.shift_right_logical(r, int(math.log2(per)))
    return (s == src).astype(BF16)


def _band_attention(q, kk, vv, bias):
    return _band_finish(_dot_nt(q, kk), vv, bias)


def _band_finish(s, vv, bias):
    s = s + bias
    m = jnp.max(s, axis=-1, keepdims=True)
    p = jnp.exp2(s - m)
    l = jnp.sum(p, axis=-1, keepdims=True)
    o = jnp.dot(p.astype(BF16), vv, preferred_element_type=F32)
    return o * (1.0 / l), m + jnp.log2(l)


def _run_pipelined(tasks):
    pending = tasks[0][0]()
    for i, (_, finish) in enumerate(tasks):
        nxt = tasks[i + 1][0]() if i + 1 < len(tasks) else None
        finish(pending)
        pending = nxt


def _attn_kernel(q_ref, k_ref, v_ref, o_ref, q4, k4, v4, q16, k16, v16, kp1, vp1, bias_scr, o_scr, l_scr):
    first = pl.program_id(2) == 0
    nblk = SPAN // ATTN_BLK
    ngrp = SPAN // PERM_ROWS

    @pl.when(first)
    def _():
        kp1[...] = jnp.zeros_like(kp1)
        vp1[...] = jnp.zeros_like(vp1)
        for buf in (k4, v4, k16, v16):
            buf[:, 0:ATTN_BLK, :] = jnp.zeros((buf.shape[0], ATTN_BLK, HEAD_DIM), BF16)

    qi = lax.broadcasted_iota(jnp.int32, (ATTN_BLK, 2 * ATTN_BLK), 0)
    kj = lax.broadcasted_iota(jnp.int32, (ATTN_BLK, 2 * ATTN_BLK), 1)
    dist = ATTN_BLK + qi - kj
    band = (dist >= 0) & (dist <= ATTN_BLK)
    bias_scr[0] = jnp.where(band, 0.0, -jnp.inf)
    first_key = jnp.where(first, ATTN_BLK, 0)
    bias_scr[1] = jnp.where(band & (kj >= first_key), 0.0, -jnp.inf)

    perm_tasks = []
    for dil, qd, kd, vd in ((DILATIONS[1], q4, k4, v4), (DILATIONS[2], q16, k16, v16)):
        perm = _perm_matrix(dil)
        per = PERM_ROWS // dil

        def scatter(dst, y, g, off, dil=dil, per=per):
            for c in range(dil):
                dst[c, off + g * per:off + (g + 1) * per, :] = y[c * per:(c + 1) * per, :]

        def rows(g):
            return slice(g * PERM_ROWS, (g + 1) * PERM_ROWS)

        def issue(a, ga, b, gb, perm=perm):
            x = jnp.concatenate([a[0, 0, 0, rows(ga), :], b[0, 0, 0, rows(gb), :]], axis=1)
            return jnp.dot(perm, x, preferred_element_type=F32)

        def finish(y, da, ga, db, gb, off, scatter=scatter):
            y = y.astype(BF16)
            scatter(da, y[:, :HEAD_DIM], ga, off)
            scatter(db, y[:, HEAD_DIM:], gb, off)

        for g in range(ngrp):
            perm_tasks.append((lambda g=g, issue=issue: issue(k_ref, g, v_ref, g),
                               lambda y, g=g, finish=finish, kd=kd, vd=vd: finish(y, kd, g, vd, g, ATTN_BLK)))
        for g in range(0, ngrp, 2):
            perm_tasks.append((lambda g=g, issue=issue: issue(q_ref, g, q_ref, g + 1),
                               lambda y, g=g, finish=finish, qd=qd: finish(y, qd, g, qd, g + 1, 0)))

    def attn_task(get_q, get_kk, get_vv, bias_idx, branch, start, stride):
        def finish(s):
            o, lse = _band_finish(s, get_vv(), bias_scr[bias_idx])
            rws = pl.ds(start, ATTN_BLK, stride=stride) if stride > 1 else pl.ds(start, ATTN_BLK)
            o_scr[branch, rws, :] = o
            l_scr[branch, rws, :] = jnp.broadcast_to(lse, (ATTN_BLK, HEAD_DIM))
        return (lambda: _dot_nt(get_q(), get_kk()), finish)

    nat_tasks = []
    for r in range(nblk):
        lo = r * ATTN_BLK
        if r == 0:
            get_kk = lambda: jnp.concatenate([kp1[...], k_ref[0, 0, 0, 0:ATTN_BLK, :]], axis=0)
            get_vv = lambda: jnp.concatenate([vp1[...], v_ref[0, 0, 0, 0:ATTN_BLK, :]], axis=0)
        else:
            get_kk = lambda lo=lo: k_ref[0, 0, 0, lo - ATTN_BLK:lo + ATTN_BLK, :]
            get_vv = lambda lo=lo: v_ref[0, 0, 0, lo - ATTN_BLK:lo + ATTN_BLK, :]
        nat_tasks.append(attn_task(lambda lo=lo: q_ref[0, 0, 0, lo:lo + ATTN_BLK, :], get_kk, get_vv,
                                   1 if r == 0 else 0, 0, lo, 1))

    cls_tasks = []
    dil = DILATIONS[1]
    for c in range(dil):
        for a in range(nblk // dil):
            lo = a * ATTN_BLK
            cls_tasks.append(attn_task(lambda c=c, lo=lo: q4[c, lo:lo + ATTN_BLK, :],
                                       lambda c=c, lo=lo: k4[c, lo:lo + 2 * ATTN_BLK, :],
                                       lambda c=c, lo=lo: v4[c, lo:lo + 2 * ATTN_BLK, :],
                                       1 if a == 0 else 0, 1, lo * dil + c, dil))
    dil = DILATIONS[2]
    for c in range(dil):
        cls_tasks.append(attn_task(lambda c=c: q16[c], lambda c=c: k16[c], lambda c=c: v16[c], 1, 2, c, dil))

    always = pl.program_id(2) >= 0
    att = nat_tasks + cls_tasks
    grp_sz = 4
    for grp in [perm_tasks[:12], perm_tasks[12:]] + [att[i:i + grp_sz] for i in range(0, len(att), grp_sz)]:
        pl.when(always)(lambda grp=grp: _run_pipelined(grp))

    kp1[...] = k_ref[0, 0, 0, SPAN - ATTN_BLK:SPAN, :]
    vp1[...] = v_ref[0, 0, 0, SPAN - ATTN_BLK:SPAN, :]
    for buf in (k4, v4, k16, v16):
        n = buf.shape[1]
        buf[:, 0:ATTN_BLK, :] = buf[:, n - ATTN_BLK:n, :]

    for r in range(nblk):
        rows = slice(r * ATTN_BLK, (r + 1) * ATTN_BLK)
        l0, l1, l2 = l_scr[0, rows, :], l_scr[1, rows, :], l_scr[2, rows, :]
        mx = jnp.maximum(jnp.maximum(l0, l1), l2)
        w0, w1, w2 = jnp.exp2(l0 - mx), jnp.exp2(l1 - mx), jnp.exp2(l2 - mx)
        acc = w0 * o_scr[0, rows, :] + w1 * o_scr[1, rows, :] + w2 * o_scr[2, rows, :]
        o_ref[0, rows, :] = (acc * (1.0 / (w0 + w1 + w2))).astype(o_ref.dtype)


def _dilated_attention(main, bsz, seq):
    nspan = seq // SPAN
    d4, d16 = DILATIONS[1], DILATIONS[2]
    blk = (1, 1, 1, SPAN, HEAD_DIM)
    spec = lambda slot: pl.BlockSpec(blk, lambda b, h, n, s=slot: (s, b, h, n, 0))
    cls = lambda dil, halo: pltpu.VMEM((dil, halo + SPAN // dil, HEAD_DIM), BF16)
    return pl.pallas_call(
        _attn_kernel,
        grid=(bsz, N_HEADS, nspan),
        in_specs=[spec(_SLOT_Q), spec(_SLOT_K), spec(_SLOT_V)],
        out_specs=pl.BlockSpec((1, SPAN, HEAD_DIM), lambda b, h, n: (b, n, h)),
        out_shape=jax.ShapeDtypeStruct((bsz, seq, GROUP_COLS), BF16),
        scratch_shapes=[cls(d4, 0), cls(d4, ATTN_BLK), cls(d4, ATTN_BLK),
                        cls(d16, 0), cls(d16, ATTN_BLK), cls(d16, ATTN_BLK),
                        pltpu.VMEM((ATTN_BLK, HEAD_DIM), BF16), pltpu.VMEM((ATTN_BLK, HEAD_DIM), BF16),
                        pltpu.VMEM((2, ATTN_BLK, 2 * ATTN_BLK), F32),
                        pltpu.VMEM((3, SPAN, HEAD_DIM), F32), pltpu.VMEM((3, SPAN, HEAD_DIM), F32)],
        compiler_params=_params(("arbitrary", "arbitrary", "arbitrary")),
        name="dilated_attn",
    )(main, main, main)


def _split3(x):
    hi = x.astype(BF16)
    r1 = x - hi.astype(F32)
    mid = r1.astype(BF16)
    lo = (r1 - mid.astype(F32)).astype(BF16)
    return hi, mid, lo


def _hgrn_chunks(qs, vs, gates, gs, lbs, nw, sts, consts, cs_scr):
    tril3, xor_ts, causal, lane_mod, levels = consts
    c_rows = HG_CHUNK
    heads = range(len(qs))

    bs, css = [], []
    for h in heads:
        sg = 1.0 / (1.0 + jnp.exp(-gs[h]))
        f = lbs[h] + (1.0 - lbs[h]) * sg
        l2k = jnp.log2(1.0 - f)
        hi, mid, lo = _split3(jnp.log2(f))
        b = jnp.dot(tril3, jnp.concatenate([hi, mid, lo], axis=0), preferred_element_type=F32)
        bs.append(b)
        css.append(b - l2k)
        cs_scr[h] = css[h]
    b_lasts = [b[c_rows - 1:c_rows, :] for b in bs]

    os_ = [_dot_nt((qs[h] * jnp.exp2(bs[h])).astype(BF16), sts[h].astype(BF16)) for h in heads]

    a_mats = [None for _ in heads]
    for sz in levels:
        zero = jnp.zeros((sz, HEAD_DIM), F32)
        for h in heads:
            q, b, cs = qs[h], bs[h], css[h]
            qparts, kparts = [], []
            for p in range(c_rows // (2 * sz)):
                l0, r0, r1 = p * 2 * sz, p * 2 * sz + sz, (p + 1) * 2 * sz
                bref = b[r0 - 1:r0, :]
                kparts += [jnp.exp2(bref - cs[l0:r0, :]), zero]
                qparts += [zero, q[r0:r1, :] * jnp.exp2(b[r0:r1, :] - bref)]
            al = _dot_nt(jnp.concatenate(qparts, axis=0).astype(BF16),
                         jnp.concatenate(kparts, axis=0).astype(BF16))
            a_mats[h] = al if a_mats[h] is None else jnp.where(xor_ts < sz, a_mats[h], al)

    for h in heads:
        q, b, cs = qs[h], bs[h], css[h]
        blocks = []
        for i in range(c_rows // HG_SUB):
            rs = slice(i * HG_SUB, (i + 1) * HG_SUB)
            bb, qq = b[rs, :], q[rs, :]
            ablk = jnp.zeros((HG_SUB, c_rows), F32)
            for s in range(HG_SUB):
                p = jnp.exp2(bb - cs_scr[h, i * HG_SUB + s:i * HG_SUB + s + 1, :]) * qq
                ablk = jnp.where(lane_mod[s], jnp.sum(p, axis=-1, keepdims=True), ablk)
            blocks.append(ablk)
        diag = jnp.where(causal, jnp.concatenate(blocks, axis=0), 0.0)
        a_mats[h] = jnp.where(xor_ts < HG_SUB, diag, a_mats[h]).astype(BF16)

    os_ = [os_[h] + jnp.dot(a_mats[h], vs[h], preferred_element_type=F32) for h in heads]

    st_new = []
    for h in heads:
        kdec = jnp.exp2(b_lasts[h] - css[h]).astype(BF16)
        upd = jnp.dot(vs[h].astype(F32).T.astype(BF16), kdec, preferred_element_type=F32)
        st_new.append(sts[h] * jnp.exp2(b_lasts[h]) + upd)

    ys = [(_rms_rows(os_[h]) * nw * gates[h].astype(F32)).astype(BF16) for h in heads]
    return list(zip(ys, st_new))


def _hgrn_kernel(q_ref, i_ref, g_ref, f_ref, lbl_ref, nw_ref, o_ref, st_scr, cs_scr):
    c_rows = HG_CHUNK
    nh = q_ref.shape[2]

    @pl.when(pl.program_id(2) == 0)
    def _():
        st_scr[...] = jnp.zeros_like(st_scr)

    lg = lbl_ref[...]
    e = jnp.exp(lg - jnp.max(lg, axis=0, keepdims=True))
    lb_all = e[0:1, :] / jnp.sum(e, axis=0, keepdims=True)

    ti = lax.broadcasted_iota(jnp.int32, (c_rows, c_rows), 0)
    si = lax.broadcasted_iota(jnp.int32, (c_rows, c_rows), 1)
    tril = (ti >= si).astype(BF16)
    levels = []
    sz = HG_SUB
    while sz < c_rows:
        levels.append(sz)
        sz *= 2
    lane_in_sub = lax.broadcasted_iota(jnp.int32, (HG_SUB, c_rows), 1) & (HG_SUB - 1)
    lane_mod = tuple(lane_in_sub == s for s in range(HG_SUB))
    consts = (jnp.concatenate([tril, tril, tril], axis=1), ti ^ si, ti >= si, lane_mod, tuple(levels))

    def chunk(ci, carry):
        r0 = pl.multiple_of(ci * c_rows, c_rows)
        rows = pl.ds(r0, c_rows)
        hs = range(nh)
        outs = _hgrn_chunks([q_ref[0, 0, h, rows, :].astype(F32) for h in hs],
                            [i_ref[0, 0, h, rows, :] for h in hs],
                            [g_ref[0, 0, h, rows, :] for h in hs],
                            [f_ref[0, h, rows, :] for h in hs],
                            [lb_all[:, h * HEAD_DIM:(h + 1) * HEAD_DIM] for h in hs],
                            nw_ref[...], [st_scr[h] for h in hs], consts, cs_scr)
        for hh, (y, st_new) in enumerate(outs):
            st_scr[hh] = st_new
            o_ref[0, rows, hh * HEAD_DIM:(hh + 1) * HEAD_DIM] = y
        return carry

    lax.fori_loop(0, o_ref.shape[1] // c_rows, chunk, 0)


def _hgrn2(main, gf, lb_logits, hg_norm_w, bsz, seq):
    tt = 2048
    nh = HG_HEADS_PER_STEP
    blk = (1, 1, nh, tt, HEAD_DIM)
    spec = lambda slot: pl.BlockSpec(blk, lambda b, h, n, s=slot: (s, b, h, n, 0))
    return pl.pallas_call(
        _hgrn_kernel,
        grid=(bsz, N_HEADS // nh, seq // tt),
        in_specs=[spec(_SLOT_HQ), spec(_SLOT_HI), spec(_SLOT_HG),
                  pl.BlockSpec((1, nh, tt, HEAD_DIM), lambda b, h, n: (b, h, n, 0)),
                  pl.BlockSpec((lb_logits.shape[0], nh * HEAD_DIM), lambda b, h, n: (0, h)),
                  pl.BlockSpec((1, HEAD_DIM), lambda b, h, n: (0, 0))],
        out_specs=pl.BlockSpec((1, tt, nh * HEAD_DIM), lambda b, h, n: (b, n, h)),
        out_shape=jax.ShapeDtypeStruct((bsz, seq, GROUP_COLS), BF16),
        scratch_shapes=[pltpu.VMEM((nh, HEAD_DIM, HEAD_DIM), F32), pltpu.VMEM((nh, HG_CHUNK, HEAD_DIM), F32)],
        compiler_params=_params(("arbitrary", "arbitrary", "arbitrary")),
        name="hgrn2",
    )(main, main, main, gf, lb_logits, hg_norm_w.reshape(1, HEAD_DIM))


def _outproj_kernel(x_ref, a_ref, g_ref, mod_ref, aw_ref, w_ref, o_ref):
    half = a_ref.shape[1]
    an = (_rms_rows(a_ref[...].astype(F32)) * aw_ref[...]).astype(BF16)
    mix = (jnp.dot(an, w_ref[0:half, :], preferred_element_type=F32)
           + jnp.dot(g_ref[...], w_ref[half:, :], preferred_element_type=F32))
    o_ref[...] = x_ref[...] + mod_ref[0, 2:3, :] * mix


def _out_projection(x2, attn2, hg2, mod3, attn_norm_w, w_out_bf, seq):
    t, d = x2.shape
    half = attn2.shape[1]
    tm = 512
    rpb = seq // tm
    row = lambda i: (i, 0)
    return pl.pallas_call(
        _outproj_kernel,
        grid=(t // tm,),
        in_specs=[pl.BlockSpec((tm, d), row),
                  pl.BlockSpec((tm, half), row),
                  pl.BlockSpec((tm, half), row),
                  pl.BlockSpec((1, 6, d), lambda i: (i // rpb, 0, 0)),
                  pl.BlockSpec((1, half), lambda i: (0, 0)),
                  pl.BlockSpec((2 * half, d), lambda i: (0, 0))],
        out_specs=pl.BlockSpec((tm, d), row),
        out_shape=jax.ShapeDtypeStruct((t, d), F32),
        compiler_params=_params(("arbitrary",)),
        name="out_proj",
    )(x2, attn2, hg2, mod3, attn_norm_w.reshape(1, half), w_out_bf)


_MLP_STEPS = 8


def _mlp_kernel(x_ref, xn_ref, mod_ref, modn_ref, nw_ref, w1_ref, w2_ref, o_ref, h0_scr, h1_scr, acc_scr):
    i, k = pl.program_id(0), pl.program_id(1)
    nk = pl.num_programs(1)

    def modulated(x, mod):
        y = _rms_rows(x) * nw_ref[...]
        return (y * (1.0 + mod[0, 4:5, :]) + mod[0, 3:4, :]).astype(BF16)

    @pl.when((i == 0) & (k == 0))
    def _():
        h0_scr[...] = modulated(x_ref[...], mod_ref)

    @pl.when(k == 0)
    def _():
        acc_scr[...] = jnp.zeros_like(acc_scr)

    def step(h_cur, h_nxt):
        n_rows = x_ref.shape[0] // _MLP_STEPS
        r0 = pl.multiple_of(k * n_rows, n_rows)
        h_nxt[pl.ds(r0, n_rows), :] = modulated(xn_ref[pl.ds(r0, n_rows), :], modn_ref)

        a = jnp.dot(h_cur[...], w1_ref[...], preferred_element_type=F32)
        a = jnp.square(jnp.maximum(a, 0.0)).astype(BF16)
        acc_scr[...] += jnp.dot(a, w2_ref[...], preferred_element_type=F32)

    even = lax.rem(i, 2) == 0
    pl.when(even)(lambda: step(h0_scr, h1_scr))
    pl.when(jnp.logical_not(even))(lambda: step(h1_scr, h0_scr))

    @pl.when(k == nk - 1)
    def _():
        o_ref[...] = x_ref[...] + mod_ref[0, 5:6, :] * acc_scr[...]


def _mlp(x2, mod3, norm2_w, w1_bf, w2_bf, seq):
    t, d = x2.shape
    dff = w1_bf.shape[1]
    tm = 512
    tf = dff // _MLP_STEPS
    rpb = seq // tm
    last = t // tm - 1
    row = lambda i, k: (i, 0)
    nxt = lambda i, k: (jnp.minimum(i + 1, last), 0)
    return pl.pallas_call(
        _mlp_kernel,
        grid=(t // tm, dff // tf),
        in_specs=[pl.BlockSpec((tm, d), row),
                  pl.BlockSpec((tm, d), nxt),
                  pl.BlockSpec((1, 6, d), lambda i, k: (i // rpb, 0, 0)),
                  pl.BlockSpec((1, 6, d), lambda i, k: (jnp.minimum(i + 1, last) // rpb, 0, 0)),
                  pl.BlockSpec((1, d), lambda i, k: (0, 0)),
                  pl.BlockSpec((d, tf), lambda i, k: (0, k)),
                  pl.BlockSpec((tf, d), lambda i, k: (k, 0))],
        out_specs=pl.BlockSpec((tm, d), row),
        out_shape=jax.ShapeDtypeStruct((t, d), F32),
        scratch_shapes=[pltpu.VMEM((tm, d), BF16), pltpu.VMEM((tm, d), BF16), pltpu.VMEM((tm, d), F32)],
        compiler_params=_params(("arbitrary", "arbitrary")),
        name="mlp",
    )(x2, x2, mod3, mod3, norm2_w.reshape(1, d), w1_bf, w2_bf)


def kernel(x, c, positions, norm1_w, w_ada, b_ada, w_in, q_norm_w, k_norm_w, attn_out_norm_w,
           hg_lb_logits, hg_norm_w, w_out, norm2_w, w_ff1, w_ff2):
    bsz, seq, d = x.shape
    assert w_ada.shape[0] == 1, "single-layer block"
    assert seq % SPAN == 0 and d == 2 * GROUP_COLS and w_in.shape[2] == _N_GROUPS * GROUP_COLS
    x2 = x.reshape(bsz * seq, d)

    mod3 = _ada_mod(c, w_ada[0], b_ada[0]).reshape(bsz, 6, d)
    tabs = _rope_tables(positions)

    main, gf = _in_projection(x2, mod3, norm1_w[0], w_in[0], q_norm_w[0], k_norm_w[0], tabs, bsz, seq)
    attn = _dilated_attention(main, bsz, seq)
    hg = _hgrn2(main, gf, hg_lb_logits, hg_norm_w[0], bsz, seq)

    x1 = _out_projection(x2, attn.reshape(bsz * seq, GROUP_COLS), hg.reshape(bsz * seq, GROUP_COLS), mod3,
                         attn_out_norm_w[0], w_out[0].astype(BF16), seq)
    out = _mlp(x1, mod3, norm2_w[0], w_ff1[0].astype(BF16), w_ff2[0].astype(BF16), seq)
    return out.reshape(bsz, seq, d)
```

```python
import math

import jax
import jax.numpy as jnp
from jax import lax
from jax.experimental import pallas as pl
from jax.experimental.pallas import tpu as pltpu

F32 = jnp.float32
BF16 = jnp.bfloat16

HEAD_DIM = 128
N_HEADS = 8
GROUP_COLS = N_HEADS * HEAD_DIM
ROPE_THETA = 500000.0
ROPE_DIM = HEAD_DIM // 4
ROPE_HALF = ROPE_DIM // 2
DILATIONS = (1, 4, 16)
ATTN_BLK = 128
SPAN = ATTN_BLK * DILATIONS[-1]
PERM_ROWS = 256
HG_CHUNK = 128
HG_SUB = 8
HG_HEADS_PER_STEP = 8
EPS = 1e-6
LOG2E = math.log2(math.e)
VMEM_LIMIT = 56 * 1024 * 1024

_N_GROUPS = 7
_SLOT_Q, _SLOT_K, _SLOT_V, _SLOT_HQ, _SLOT_HI, _SLOT_HG = range(6)


def _params(sem, vmem=VMEM_LIMIT):
    return pltpu.CompilerParams(dimension_semantics=sem, vmem_limit_bytes=vmem)


def _silu(x):
    return x * (0.5 * jnp.tanh(0.5 * x) + 0.5)


def _rms_rows(x, eps=EPS):
    return x * lax.rsqrt(jnp.mean(x * x, axis=-1, keepdims=True) + eps)


NORM_ROWS = 32


def _norm_modulate(x_ref, h_ref, gain, scale, shift):
    g = gain * (1.0 + scale)
    for c in range(x_ref.shape[0] // NORM_ROWS):
        rows = slice(c * NORM_ROWS, (c + 1) * NORM_ROWS)
        h_ref[rows, :] = (_rms_rows(x_ref[rows, :]) * g + shift).astype(BF16)


def _dot_nt(a, b):
    return lax.dot_general(a, b, (((1,), (1,)), ((), ())), preferred_element_type=F32)


def _ada_kernel(c_ref, w_ref, b_ref, o_ref):
    sc = _silu(c_ref[...]).astype(BF16)
    o_ref[...] = jnp.dot(sc, w_ref[...].astype(BF16), preferred_element_type=F32) + b_ref[...]


def _ada_mod(c, w_ada, b_ada):
    bsz, d = c.shape
    n = w_ada.shape[1]
    tn = 1024
    rows = 8
    c8 = jnp.pad(c, ((0, rows - bsz), (0, 0)))
    out = pl.pallas_call(
        _ada_kernel,
        grid=(n // tn,),
        in_specs=[pl.BlockSpec((rows, d), lambda j: (0, 0)),
                  pl.BlockSpec((d, tn), lambda j: (0, j)),
                  pl.BlockSpec((1, tn), lambda j: (0, j))],
        out_specs=pl.BlockSpec((rows, tn), lambda j: (0, j)),
        out_shape=jax.ShapeDtypeStruct((rows, n), F32),
        compiler_params=_params(("arbitrary",)),
        name="ada_mod",
    )(c8, w_ada, b_ada.reshape(1, n))
    return out[:bsz]


def _rope_kernel(pos_ref, invf_ref, cos_ref, sin_ref):
    ang = pos_ref[...] * invf_ref[...]
    cos_ref[...] = jnp.cos(ang)
    sin_ref[...] = jnp.sin(ang)


def _rope_tables(positions):
    t = positions.size
    ts = 2048
    inv_freq = ROPE_THETA ** (-(jnp.arange(ROPE_HALF, dtype=F32) * 2.0) / ROPE_DIM)
    invf = jnp.broadcast_to(inv_freq[:, None], (ROPE_HALF, ts))
    pos = positions.astype(F32).reshape(1, t)
    tab = jax.ShapeDtypeStruct((ROPE_HALF, t), F32)
    spec = pl.BlockSpec((ROPE_HALF, ts), lambda i: (0, i))
    return pl.pallas_call(
        _rope_kernel,
        grid=(t // ts,),
        in_specs=[pl.BlockSpec((1, ts), lambda i: (0, i)),
                  pl.BlockSpec((ROPE_HALF, ts), lambda i: (0, 0))],
        out_specs=[spec, spec],
        out_shape=[tab, tab],
        compiler_params=_params(("arbitrary",)),
        name="rope_tables",
    )(pos, invf)


_PROJ_COLS = 2 * HEAD_DIM
_QK_TOKENS = 256


def _inproj_kernel(x_ref, mod_ref, n1w_ref, w_ref, qg_ref, kg_ref, cos_ref, sin_ref,
                   main_ref, gf_ref, h_scr):
    j = pl.program_id(1)

    @pl.when(j == 0)
    def _():
        _norm_modulate(x_ref, h_scr, n1w_ref[...], mod_ref[0, 1:2, :], mod_ref[0, 0:1, :])

    def heads(fn):
        for hp in range(GROUP_COLS // _PROJ_COLS):
            res = jnp.dot(h_scr[...], w_ref[:, hp * _PROJ_COLS:(hp + 1) * _PROJ_COLS],
                          preferred_element_type=F32)
            for hh in range(_PROJ_COLS // HEAD_DIM):
                fn(hp * (_PROJ_COLS // HEAD_DIM) + hh, res[:, hh * HEAD_DIM:(hh + 1) * HEAD_DIM])

    @pl.when(j <= _SLOT_K)
    def _():
        gain = jnp.where(j == _SLOT_Q, qg_ref[...], kg_ref[...])
        nt = _QK_TOKENS
        gain = jnp.concatenate([gain] * (nt // HEAD_DIM), axis=1)
        w_t = w_ref[...].T
        for part in range(h_scr.shape[0] // nt):
            toks = slice(part * nt, (part + 1) * nt)
            rt = _dot_nt(w_t, h_scr[toks, :])
            cos, sin = cos_ref[:, toks], sin_ref[:, toks]
            for h in range(N_HEADS):
                r = rt[h * HEAD_DIM:(h + 1) * HEAD_DIM, :]
                y = r * lax.rsqrt(jnp.mean(r * r, axis=0, keepdims=True) + EPS) * gain
                x1, x2 = y[0:ROPE_HALF, :], y[ROPE_HALF:ROPE_DIM, :]
                y = jnp.concatenate([x1 * cos - x2 * sin, x2 * cos + x1 * sin, y[ROPE_DIM:, :]], axis=0)
                main_ref[0, 0, h, toks, :] = y.T.astype(BF16)

    @pl.when((j == _SLOT_V) | (j == _SLOT_HI))
    def _():
        def fn(h, r):
            main_ref[0, 0, h] = r.astype(BF16)
        heads(fn)

    @pl.when(j == _SLOT_HQ)
    def _():
        def fn(h, r):
            main_ref[0, 0, h] = (_silu(r) * (HEAD_DIM ** -0.5)).astype(BF16)
        heads(fn)

    @pl.when(j == _SLOT_HG)
    def _():
        def fn(h, r):
            main_ref[0, 0, h] = _silu(r).astype(BF16)
        heads(fn)

    @pl.when(j == _N_GROUPS - 1)
    def _():
        def fn(h, r):
            gf_ref[0, h] = r
        heads(fn)


def _in_projection(x2, mod3, norm1_w, w_in, qnw, knw, tabs, bsz, seq):
    t, d = x2.shape
    tm = 1024
    rpb = seq // tm
    w_bf = w_in.astype(BF16)
    lanes = lambda g: jnp.broadcast_to(g[:, None], (HEAD_DIM, HEAD_DIM))
    qg = lanes(qnw * (HEAD_DIM ** -0.5 * LOG2E))
    kg = lanes(knw)

    def wcol(i, j):
        return (0, jnp.where(j <= 3, j, jnp.where(j == 4, 5, jnp.where(j == 5, 6, 4))))

    row = lambda i, j: (i, 0)
    const = lambda i, j: (0, 0)
    main_shape = jax.ShapeDtypeStruct((6, bsz, N_HEADS, seq, HEAD_DIM), BF16)
    gf_shape = jax.ShapeDtypeStruct((bsz, N_HEADS, seq, HEAD_DIM), F32)
    return pl.pallas_call(
        _inproj_kernel,
        grid=(t // tm, _N_GROUPS),
        in_specs=[pl.BlockSpec((tm, d), row),
                  pl.BlockSpec((1, 6, d), lambda i, j: (i // rpb, 0, 0)),
                  pl.BlockSpec((1, d), const),
                  pl.BlockSpec((d, GROUP_COLS), wcol),
                  pl.BlockSpec((HEAD_DIM, HEAD_DIM), const),
                  pl.BlockSpec((HEAD_DIM, HEAD_DIM), const),
                  pl.BlockSpec((ROPE_HALF, tm), lambda i, j: (0, i)),
                  pl.BlockSpec((ROPE_HALF, tm), lambda i, j: (0, i))],
        out_specs=[pl.BlockSpec((1, 1, N_HEADS, tm, HEAD_DIM),
                                lambda i, j: (jnp.minimum(j, 5), i // rpb, 0, i % rpb, 0)),
                   pl.BlockSpec((1, N_HEADS, tm, HEAD_DIM), lambda i, j: (i // rpb, 0, i % rpb, 0))],
        out_shape=[main_shape, gf_shape],
        scratch_shapes=[pltpu.VMEM((tm, d), BF16)],
        compiler_params=_params(("arbitrary", "arbitrary")),
        name="in_proj",
    )(x2, mod3, norm1_w.reshape(1, d), w_bf, qg, kg, *tabs)


def _perm_matrix(dil):
    per = PERM_ROWS // dil
    r = lax.broadcasted_iota(jnp.int32, (PERM_ROWS, PERM_ROWS), 0)
    s = lax.broadcasted_iota(jnp.int32, (PERM_ROWS, PERM_ROWS), 1)
    src = (r & (per - 1)) * dil + lax.shift_right_logical(r, int(math.log2(per)))
    return (s == src).astype(BF16)


def _band_attention(q, kk, vv, bias):
    return _band_finish(_dot_nt(q, kk), vv, bias)


def _band_finish(s, vv, bias):
    s = s + bias
    m = jnp.max(s, axis=-1, keepdims=True)
    p = jnp.exp2(s - m).astype(BF16)
    ones = jnp.ones((vv.shape[0], HEAD_DIM), BF16)
    ol = jnp.dot(p, jnp.concatenate([vv, ones], axis=1), preferred_element_type=F32)
    o, l = ol[:, :HEAD_DIM], ol[:, HEAD_DIM:]
    return o * (1.0 / l), m + jnp.log2(l)


def _run_pipelined(tasks):
    pending = tasks[0][0]()
    for i, (_, finish) in enumerate(tasks):
        nxt = tasks[i + 1][0]() if i + 1 < len(tasks) else None
        finish(pending)
        pending = nxt


def _attn_kernel(q_ref, k_ref, v_ref, o_ref, q4, k4, v4, q16, k16, v16, kp1, vp1, bias_scr, o_scr, l_scr):
    first = pl.program_id(2) == 0
    nblk = SPAN // ATTN_BLK
    ngrp = SPAN // PERM_ROWS

    @pl.when(first)
    def _():
        kp1[...] = jnp.zeros_like(kp1)
        vp1[...] = jnp.zeros_like(vp1)
        for buf in (k4, v4, k16, v16):
            buf[:, 0:ATTN_BLK, :] = jnp.zeros((buf.shape[0], ATTN_BLK, HEAD_DIM), BF16)

    qi = lax.broadcasted_iota(jnp.int32, (ATTN_BLK, 2 * ATTN_BLK), 0)
    kj = lax.broadcasted_iota(jnp.int32, (ATTN_BLK, 2 * ATTN_BLK), 1)
    dist = ATTN_BLK + qi - kj
    band = (dist >= 0) & (dist <= ATTN_BLK)
    bias_scr[0] = jnp.where(band, 0.0, -jnp.inf)
    first_key = jnp.where(first, ATTN_BLK, 0)
    bias_scr[1] = jnp.where(band & (kj >= first_key), 0.0, -jnp.inf)

    perm_tasks = []
    for dil, qd, kd, vd in ((DILATIONS[1], q4, k4, v4), (DILATIONS[2], q16, k16, v16)):
        perm = _perm_matrix(dil)
        per = PERM_ROWS // dil

        def scatter(dst, y, g, off, dil=dil, per=per):
            for c in range(dil):
                dst[c, off + g * per:off + (g + 1) * per, :] = y[c * per:(c + 1) * per, :]

        def rows(g):
            return slice(g * PERM_ROWS, (g + 1) * PERM_ROWS)

        def issue(a, ga, b, gb, perm=perm):
            x = jnp.concatenate([a[0, 0, 0, rows(ga), :], b[0, 0, 0, rows(gb), :]], axis=1)
            return jnp.dot(perm, x, preferred_element_type=F32)

        def finish(y, da, ga, db, gb, off, scatter=scatter):
            y = y.astype(BF16)
            scatter(da, y[:, :HEAD_DIM], ga, off)
            scatter(db, y[:, HEAD_DIM:], gb, off)

        for g in range(ngrp):
            perm_tasks.append((lambda g=g, issue=issue: issue(k_ref, g, v_ref, g),
                               lambda y, g=g, finish=finish, kd=kd, vd=vd: finish(y, kd, g, vd, g, ATTN_BLK)))
        for g in range(0, ngrp, 2):
            perm_tasks.append((lambda g=g, issue=issue: issue(q_ref, g, q_ref, g + 1),
                               lambda y, g=g, finish=finish, qd=qd: finish(y, qd, g, qd, g + 1, 0)))

    def attn_task(get_q, get_kk, get_vv, bias_idx, branch, start, stride):
        def finish(s):
            o, lse = _band_finish(s, get_vv(), bias_scr[bias_idx])
            rws = pl.ds(start, ATTN_BLK, stride=stride) if stride > 1 else pl.ds(start, ATTN_BLK)
            o_scr[branch, rws, :] = o
            l_scr[branch, rws, :] = jnp.broadcast_to(lse, (ATTN_BLK, HEAD_DIM))
        return (lambda: _dot_nt(get_q(), get_kk()), finish)

    nat_tasks = []
    for r in range(nblk):
        lo = r * ATTN_BLK
        if r == 0:
            get_kk = lambda: jnp.concatenate([kp1[...], k_ref[0, 0, 0, 0:ATTN_BLK, :]], axis=0)
            get_vv = lambda: jnp.concatenate([vp1[...], v_ref[0, 0, 0, 0:ATTN_BLK, :]], axis=0)
        else:
            get_kk = lambda lo=lo: k_ref[0, 0, 0, lo - ATTN_BLK:lo + ATTN_BLK, :]
            get_vv = lambda lo=lo: v_ref[0, 0, 0, lo - ATTN_BLK:lo + ATTN_BLK, :]
        nat_tasks.append(attn_task(lambda lo=lo: q_ref[0, 0, 0, lo:lo + ATTN_BLK, :], get_kk, get_vv,
                                   1 if r == 0 else 0, 0, lo, 1))

    cls_tasks = []
    dil = DILATIONS[1]
    for c in range(dil):
        for a in range(nblk // dil):
            lo = a * ATTN_BLK
            cls_tasks.append(attn_task(lambda c=c, lo=lo: q4[c, lo:lo + ATTN_BLK, :],
                                       lambda c=c, lo=lo: k4[c, lo:lo + 2 * ATTN_BLK, :],
                                       lambda c=c, lo=lo: v4[c, lo:lo + 2 * ATTN_BLK, :],
                                       1 if a == 0 else 0, 1, lo * dil + c, dil))
    dil = DILATIONS[2]
    for c in range(dil):
        cls_tasks.append(attn_task(lambda c=c: q16[c], lambda c=c: k16[c], lambda c=c: v16[c], 1, 2, c, dil))

    always = pl.program_id(2) >= 0
    att = nat_tasks + cls_tasks
    grp_sz = 4
    for grp in [perm_tasks[:12], perm_tasks[12:]] + [att[i:i + grp_sz] for i in range(0, len(att), grp_sz)]:
        pl.when(always)(lambda grp=grp: _run_pipelined(grp))

    kp1[...] = k_ref[0, 0, 0, SPAN - ATTN_BLK:SPAN, :]
    vp1[...] = v_ref[0, 0, 0, SPAN - ATTN_BLK:SPAN, :]
    for buf in (k4, v4, k16, v16):
        n = buf.shape[1]
        buf[:, 0:ATTN_BLK, :] = buf[:, n - ATTN_BLK:n, :]

    for r in range(nblk):
        rows = slice(r * ATTN_BLK, (r + 1) * ATTN_BLK)
        l0, l1, l2 = l_scr[0, rows, :], l_scr[1, rows, :], l_scr[2, rows, :]
        mx = jnp.maximum(jnp.maximum(l0, l1), l2)
        w0, w1, w2 = jnp.exp2(l0 - mx), jnp.exp2(l1 - mx), jnp.exp2(l2 - mx)
        acc = w0 * o_scr[0, rows, :] + w1 * o_scr[1, rows, :] + w2 * o_scr[2, rows, :]
        o_ref[0, rows, :] = (acc * (1.0 / (w0 + w1 + w2))).astype(o_ref.dtype)


def _dilated_attention(main, bsz, seq):
    nspan = seq // SPAN
    d4, d16 = DILATIONS[1], DILATIONS[2]
    blk = (1, 1, 1, SPAN, HEAD_DIM)
    spec = lambda slot: pl.BlockSpec(blk, lambda b, h, n, s=slot: (s, b, h, n, 0))
    cls = lambda dil, halo: pltpu.VMEM((dil, halo + SPAN // dil, HEAD_DIM), BF16)
    return pl.pallas_call(
        _attn_kernel,
        grid=(bsz, N_HEADS, nspan),
        in_specs=[spec(_SLOT_Q), spec(_SLOT_K), spec(_SLOT_V)],
        out_specs=pl.BlockSpec((1, SPAN, HEAD_DIM), lambda b, h, n: (b, n, h)),
        out_shape=jax.ShapeDtypeStruct((bsz, seq, GROUP_COLS), BF16),
        scratch_shapes=[cls(d4, 0), cls(d4, ATTN_BLK), cls(d4, ATTN_BLK),
                        cls(d16, 0), cls(d16, ATTN_BLK), cls(d16, ATTN_BLK),
                        pltpu.VMEM((ATTN_BLK, HEAD_DIM), BF16), pltpu.VMEM((ATTN_BLK, HEAD_DIM), BF16),
                        pltpu.VMEM((2, ATTN_BLK, 2 * ATTN_BLK), F32),
                        pltpu.VMEM((3, SPAN, HEAD_DIM), F32), pltpu.VMEM((3, SPAN, HEAD_DIM), F32)],
        compiler_params=_params(("arbitrary", "arbitrary", "arbitrary")),
        name="dilated_attn",
    )(main, main, main)


def _split3(x):
    hi = x.astype(BF16)
    r1 = x - hi.astype(F32)
    mid = r1.astype(BF16)
    lo = (r1 - mid.astype(F32)).astype(BF16)
    return hi, mid, lo


def _hgrn_chunks(qs, vs, gates, gs, lbs, nw, sts, consts, cs_scr):
    tril3, xor_ts, causal, lane_mod, levels = consts
    c_rows = HG_CHUNK
    heads = range(len(qs))

    bs, css = [], []
    for h in heads:
        sg = 1.0 / (1.0 + jnp.exp(-gs[h]))
        f = lbs[h] + (1.0 - lbs[h]) * sg
        l2k = jnp.log2(1.0 - f)
        hi, mid, lo = _split3(jnp.log2(f))
        b = jnp.dot(tril3, jnp.concatenate([hi, mid, lo], axis=0), preferred_element_type=F32)
        bs.append(b)
        css.append(b - l2k)
        cs_scr[h] = css[h]
    b_lasts = [b[c_rows - 1:c_rows, :] for b in bs]

    os_ = [_dot_nt((qs[h] * jnp.exp2(bs[h])).astype(BF16), sts[h].astype(BF16)) for h in heads]

    a_mats = [None for _ in heads]
    for sz in levels:
        zero = jnp.zeros((sz, HEAD_DIM), F32)
        for h in heads:
            q, b, cs = qs[h], bs[h], css[h]
            qparts, kparts = [], []
            for p in range(c_rows // (2 * sz)):
                l0, r0, r1 = p * 2 * sz, p * 2 * sz + sz, (p + 1) * 2 * sz
                bref = b[r0 - 1:r0, :]
                kparts += [jnp.exp2(bref - cs[l0:r0, :]), zero]
                qparts += [zero, q[r0:r1, :] * jnp.exp2(b[r0:r1, :] - bref)]
            al = _dot_nt(jnp.concatenate(qparts, axis=0).astype(BF16),
                         jnp.concatenate(kparts, axis=0).astype(BF16))
            a_mats[h] = al if a_mats[h] is None else jnp.where(xor_ts < sz, a_mats[h], al)

    for h in heads:
        q, b, cs = qs[h], bs[h], css[h]
        blocks = []
        for i in range(c_rows // HG_SUB):
            rs = slice(i * HG_SUB, (i + 1) * HG_SUB)
            bb, qq = b[rs, :], q[rs, :]
            ablk = jnp.zeros((HG_SUB, c_rows), F32)
            for s in range(HG_SUB):
                p = jnp.exp2(bb - cs_scr[h, i * HG_SUB + s:i * HG_SUB + s + 1, :]) * qq
                ablk = jnp.where(lane_mod[s], jnp.sum(p, axis=-1, keepdims=True), ablk)
            blocks.append(ablk)
        diag = jnp.where(causal, jnp.concatenate(blocks, axis=0), 0.0)
        a_mats[h] = jnp.where(xor_ts < HG_SUB, diag, a_mats[h]).astype(BF16)

    os_ = [os_[h] + jnp.dot(a_mats[h], vs[h], preferred_element_type=F32) for h in heads]

    st_new = []
    for h in heads:
        kdec = jnp.exp2(b_lasts[h] - css[h]).astype(BF16)
        upd = jnp.dot(vs[h].astype(F32).T.astype(BF16), kdec, preferred_element_type=F32)
        st_new.append(sts[h] * jnp.exp2(b_lasts[h]) + upd)

    ys = [(_rms_rows(os_[h]) * nw * gates[h].astype(F32)).astype(BF16) for h in heads]
    return list(zip(ys, st_new))


def _hgrn_kernel(q_ref, i_ref, g_ref, f_ref, lbl_ref, nw_ref, o_ref, st_scr, cs_scr):
    c_rows = HG_CHUNK
    nh = q_ref.shape[2]

    @pl.when(pl.program_id(2) == 0)
    def _():
        st_scr[...] = jnp.zeros_like(st_scr)

    lg = lbl_ref[...]
    e = jnp.exp(lg - jnp.max(lg, axis=0, keepdims=True))
    lb_all = e[0:1, :] / jnp.sum(e, axis=0, keepdims=True)

    ti = lax.broadcasted_iota(jnp.int32, (c_rows, c_rows), 0)
    si = lax.broadcasted_iota(jnp.int32, (c_rows, c_rows), 1)
    tril = (ti >= si).astype(BF16)
    levels = []
    sz = HG_SUB
    while sz < c_rows:
        levels.append(sz)
        sz *= 2
    lane_in_sub = lax.broadcasted_iota(jnp.int32, (HG_SUB, c_rows), 1) & (HG_SUB - 1)
    lane_mod = tuple(lane_in_sub == s for s in range(HG_SUB))
    consts = (jnp.concatenate([tril, tril, tril], axis=1), ti ^ si, ti >= si, lane_mod, tuple(levels))

    def chunk(ci, carry):
        r0 = pl.multiple_of(ci * c_rows, c_rows)
        rows = pl.ds(r0, c_rows)
        hs = range(nh)
        outs = _hgrn_chunks([q_ref[0, 0, h, rows, :].astype(F32) for h in hs],
                            [i_ref[0, 0, h, rows, :] for h in hs],
                            [g_ref[0, 0, h, rows, :] for h in hs],
                            [f_ref[0, h, rows, :] for h in hs],
                            [lb_all[:, h * HEAD_DIM:(h + 1) * HEAD_DIM] for h in hs],
                            nw_ref[...], [st_scr[h] for h in hs], consts, cs_scr)
        for hh, (y, st_new) in enumerate(outs):
            st_scr[hh] = st_new
            o_ref[0, rows, hh * HEAD_DIM:(hh + 1) * HEAD_DIM] = y
        return carry

    lax.fori_loop(0, o_ref.shape[1] // c_rows, chunk, 0)


def _hgrn2(main, gf, lb_logits, hg_norm_w, bsz, seq):
    tt = 2048
    nh = HG_HEADS_PER_STEP
    blk = (1, 1, nh, tt, HEAD_DIM)
    spec = lambda slot: pl.BlockSpec(blk, lambda b, h, n, s=slot: (s, b, h, n, 0))
    return pl.pallas_call(
        _hgrn_kernel,
        grid=(bsz, N_HEADS // nh, seq // tt),
        in_specs=[spec(_SLOT_HQ), spec(_SLOT_HI), spec(_SLOT_HG),
                  pl.BlockSpec((1, nh, tt, HEAD_DIM), lambda b, h, n: (b, h, n, 0)),
                  pl.BlockSpec((lb_logits.shape[0], nh * HEAD_DIM), lambda b, h, n: (0, h)),
                  pl.BlockSpec((1, HEAD_DIM), lambda b, h, n: (0, 0))],
        out_specs=pl.BlockSpec((1, tt, nh * HEAD_DIM), lambda b, h, n: (b, n, h)),
        out_shape=jax.ShapeDtypeStruct((bsz, seq, GROUP_COLS), BF16),
        scratch_shapes=[pltpu.VMEM((nh, HEAD_DIM, HEAD_DIM), F32), pltpu.VMEM((nh, HG_CHUNK, HEAD_DIM), F32)],
        compiler_params=_params(("arbitrary", "arbitrary", "arbitrary")),
        name="hgrn2",
    )(main, main, main, gf, lb_logits, hg_norm_w.reshape(1, HEAD_DIM))


def _outproj_kernel(x_ref, a_ref, g_ref, mod_ref, aw_ref, w_ref, o_ref):
    half = a_ref.shape[1]
    an = (_rms_rows(a_ref[...].astype(F32)) * aw_ref[...]).astype(BF16)
    mix = (jnp.dot(an, w_ref[0:half, :], preferred_element_type=F32)
           + jnp.dot(g_ref[...], w_ref[half:, :], preferred_element_type=F32))
    o_ref[...] = x_ref[...] + mod_ref[0, 2:3, :] * mix


def _out_projection(x2, attn2, hg2, mod3, attn_norm_w, w_out_bf, seq):
    t, d = x2.shape
    half = attn2.shape[1]
    tm = 512
    rpb = seq // tm
    row = lambda i: (i, 0)
    return pl.pallas_call(
        _outproj_kernel,
        grid=(t // tm,),
        in_specs=[pl.BlockSpec((tm, d), row),
                  pl.BlockSpec((tm, half), row),
                  pl.BlockSpec((tm, half), row),
                  pl.BlockSpec((1, 6, d), lambda i: (i // rpb, 0, 0)),
                  pl.BlockSpec((1, half), lambda i: (0, 0)),
                  pl.BlockSpec((2 * half, d), lambda i: (0, 0))],
        out_specs=pl.BlockSpec((tm, d), row),
        out_shape=jax.ShapeDtypeStruct((t, d), F32),
        compiler_params=_params(("arbitrary",)),
        name="out_proj",
    )(x2, attn2, hg2, mod3, attn_norm_w.reshape(1, half), w_out_bf)


def _mlp_kernel(x_ref, mod_ref, nw_ref, w1_ref, w2_ref, o_ref, h_scr, acc_scr):
    k = pl.program_id(1)

    @pl.when(k == 0)
    def _():
        _norm_modulate(x_ref, h_scr, nw_ref[...], mod_ref[0, 4:5, :], mod_ref[0, 3:4, :])

    def chunk(first):
        a = jnp.dot(h_scr[...], w1_ref[...], preferred_element_type=F32)
        a = jnp.square(jnp.maximum(a, 0.0)).astype(BF16)
        part = jnp.dot(a, w2_ref[...], preferred_element_type=F32)
        if first:
            acc_scr[...] = part
        else:
            acc_scr[...] += part

    pl.when(k == 0)(lambda: chunk(True))
    pl.when(k > 0)(lambda: chunk(False))

    @pl.when(k == pl.num_programs(1) - 1)
    def _():
        o_ref[...] = x_ref[...] + mod_ref[0, 5:6, :] * acc_scr[...]


def _mlp(x2, mod3, norm2_w, w1_bf, w2_bf, seq):
    t, d = x2.shape
    dff = w1_bf.shape[1]
    tm, tf = 512, 1024
    rpb = seq // tm
    row = lambda i, k: (i, 0)
    return pl.pallas_call(
        _mlp_kernel,
        grid=(t // tm, dff // tf),
        in_specs=[pl.BlockSpec((tm, d), row),
                  pl.BlockSpec((1, 6, d), lambda i, k: (i // rpb, 0, 0)),
                  pl.BlockSpec((1, d), lambda i, k: (0, 0)),
                  pl.BlockSpec((d, tf), lambda i, k: (0, k)),
                  pl.BlockSpec((tf, d), lambda i, k: (k, 0))],
        out_specs=pl.BlockSpec((tm, d), row),
        out_shape=jax.ShapeDtypeStruct((t, d), F32),
        scratch_shapes=[pltpu.VMEM((tm, d), BF16), pltpu.VMEM((tm, d), F32)],
        compiler_params=_params(("arbitrary", "arbitrary")),
        name="mlp",
    )(x2, mod3, norm2_w.reshape(1, d), w1_bf, w2_bf)


def kernel(x, c, positions, norm1_w, w_ada, b_ada, w_in, q_norm_w, k_norm_w, attn_out_norm_w,
           hg_lb_logits, hg_norm_w, w_out, norm2_w, w_ff1, w_ff2):
    bsz, seq, d = x.shape
    assert w_ada.shape[0] == 1, "single-layer block"
    assert seq % SPAN == 0 and d == 2 * GROUP_COLS and w_in.shape[2] == _N_GROUPS * GROUP_COLS
    x2 = x.reshape(bsz * seq, d)

    mod3 = _ada_mod(c, w_ada[0], b_ada[0]).reshape(bsz, 6, d)
    tabs = _rope_tables(positions)

    main, gf = _in_projection(x2, mod3, norm1_w[0], w_in[0], q_norm_w[0], k_norm_w[0], tabs, bsz, seq)
    attn = _dilated_attention(main, bsz, seq)
    hg = _hgrn2(main, gf, hg_lb_logits, hg_norm_w[0], bsz, seq)

    x1 = _out_projection(x2, attn.reshape(bsz * seq, GROUP_COLS), hg.reshape(bsz * seq, GROUP_COLS), mod3,
                         attn_out_norm_w[0], w_out[0].astype(BF16), seq)
    out = _mlp(x1, mod3, norm2_w[0], w_ff1[0].astype(BF16), w_ff2[0].astype(BF16), seq)
    return out.reshape(bsz, seq, d)
```

```python
import math

import jax
import jax.numpy as jnp
from jax import lax
from jax.experimental import pallas as pl
from jax.experimental.pallas import tpu as pltpu

F32 = jnp.float32
BF16 = jnp.bfloat16

HEAD_DIM = 128
N_HEADS = 8
GROUP_COLS = N_HEADS * HEAD_DIM
ROPE_THETA = 500000.0
ROPE_DIM = HEAD_DIM // 4
ROPE_HALF = ROPE_DIM // 2
DILATIONS = (1, 4, 16)
ATTN_BLK = 128
SPAN = ATTN_BLK * DILATIONS[-1]
PERM_ROWS = 256
HG_CHUNK = 128
HG_SUB = 8
HG_HEADS_PER_STEP = 8
HG_CHUNKS_PER_ITER = 2
EPS = 1e-6
LOG2E = math.log2(math.e)
VMEM_LIMIT = 56 * 1024 * 1024

_N_GROUPS = 7
_SLOT_Q, _SLOT_K, _SLOT_V, _SLOT_HQ, _SLOT_HI, _SLOT_HG = range(6)


def _params(sem, vmem=VMEM_LIMIT):
    return pltpu.CompilerParams(dimension_semantics=sem, vmem_limit_bytes=vmem)


def _silu(x):
    return x * (0.5 * jnp.tanh(0.5 * x) + 0.5)


def _rms_rows(x, eps=EPS):
    return x * lax.rsqrt(jnp.mean(x * x, axis=-1, keepdims=True) + eps)


NORM_ROWS = 32


def _norm_modulate(x_ref, h_ref, gain, scale, shift):
    g = gain * (1.0 + scale)
    for c in range(x_ref.shape[0] // NORM_ROWS):
        rows = slice(c * NORM_ROWS, (c + 1) * NORM_ROWS)
        h_ref[rows, :] = (_rms_rows(x_ref[rows, :]) * g + shift).astype(BF16)


def _dot_nt(a, b):
    return lax.dot_general(a, b, (((1,), (1,)), ((), ())), preferred_element_type=F32)


def _ada_kernel(c_ref, w_ref, b_ref, o_ref):
    sc = _silu(c_ref[...]).astype(BF16)
    o_ref[...] = jnp.dot(sc, w_ref[...].astype(BF16), preferred_element_type=F32) + b_ref[...]


def _ada_mod(c, w_ada, b_ada):
    bsz, d = c.shape
    n = w_ada.shape[1]
    tn = 1024
    rows = 8
    c8 = jnp.pad(c, ((0, rows - bsz), (0, 0)))
    out = pl.pallas_call(
        _ada_kernel,
        grid=(n // tn,),
        in_specs=[pl.BlockSpec((rows, d), lambda j: (0, 0)),
                  pl.BlockSpec((d, tn), lambda j: (0, j)),
                  pl.BlockSpec((1, tn), lambda j: (0, j))],
        out_specs=pl.BlockSpec((rows, tn), lambda j: (0, j)),
        out_shape=jax.ShapeDtypeStruct((rows, n), F32),
        compiler_params=_params(("arbitrary",)),
        name="ada_mod",
    )(c8, w_ada, b_ada.reshape(1, n))
    return out[:bsz]


def _rope_kernel(pos_ref, invf_ref, cos_ref, sin_ref):
    ang = pos_ref[...] * invf_ref[...]
    cos_ref[...] = jnp.cos(ang)
    sin_ref[...] = jnp.sin(ang)


def _rope_tables(positions):
    t = positions.size
    ts = 2048
    inv_freq = ROPE_THETA ** (-(jnp.arange(ROPE_HALF, dtype=F32) * 2.0) / ROPE_DIM)
    invf = jnp.broadcast_to(inv_freq[:, None], (ROPE_HALF, ts))
    pos = positions.astype(F32).reshape(1, t)
    tab = jax.ShapeDtypeStruct((ROPE_HALF, t), F32)
    spec = pl.BlockSpec((ROPE_HALF, ts), lambda i: (0, i))
    return pl.pallas_call(
        _rope_kernel,
        grid=(t // ts,),
        in_specs=[pl.BlockSpec((1, ts), lambda i: (0, i)),
                  pl.BlockSpec((ROPE_HALF, ts), lambda i: (0, 0))],
        out_specs=[spec, spec],
        out_shape=[tab, tab],
        compiler_params=_params(("arbitrary",)),
        name="rope_tables",
    )(pos, invf)


_PROJ_COLS = 2 * HEAD_DIM
_QK_TOKENS = 256


def _inproj_kernel(x_ref, mod_ref, n1w_ref, w_ref, qg_ref, kg_ref, cos_ref, sin_ref,
                   main_ref, gf_ref, h_scr):
    j = pl.program_id(1)

    @pl.when(j == 0)
    def _():
        _norm_modulate(x_ref, h_scr, n1w_ref[...], mod_ref[0, 1:2, :], mod_ref[0, 0:1, :])

    def heads(fn):
        for hp in range(GROUP_COLS // _PROJ_COLS):
            res = jnp.dot(h_scr[...], w_ref[:, hp * _PROJ_COLS:(hp + 1) * _PROJ_COLS],
                          preferred_element_type=F32)
            for hh in range(_PROJ_COLS // HEAD_DIM):
                fn(hp * (_PROJ_COLS // HEAD_DIM) + hh, res[:, hh * HEAD_DIM:(hh + 1) * HEAD_DIM])

    @pl.when(j <= _SLOT_K)
    def _():
        gain = jnp.where(j == _SLOT_Q, qg_ref[...], kg_ref[...])
        nt = _QK_TOKENS
        gain = jnp.concatenate([gain] * (nt // HEAD_DIM), axis=1)
        w_t = w_ref[...].T
        for part in range(h_scr.shape[0] // nt):
            toks = slice(part * nt, (part + 1) * nt)
            rt = _dot_nt(w_t, h_scr[toks, :])
            cos, sin = cos_ref[:, toks], sin_ref[:, toks]
            for h in range(N_HEADS):
                r = rt[h * HEAD_DIM:(h + 1) * HEAD_DIM, :]
                y = r * lax.rsqrt(jnp.mean(r * r, axis=0, keepdims=True) + EPS) * gain
                x1, x2 = y[0:ROPE_HALF, :], y[ROPE_HALF:ROPE_DIM, :]
                y = jnp.concatenate([x1 * cos - x2 * sin, x2 * cos + x1 * sin, y[ROPE_DIM:, :]], axis=0)
                main_ref[0, 0, h, toks, :] = y.T.astype(BF16)

    @pl.when((j == _SLOT_V) | (j == _SLOT_HI))
    def _():
        def fn(h, r):
            main_ref[0, 0, h] = r.astype(BF16)
        heads(fn)

    @pl.when(j == _SLOT_HQ)
    def _():
        def fn(h, r):
            main_ref[0, 0, h] = (_silu(r) * (HEAD_DIM ** -0.5)).astype(BF16)
        heads(fn)

    @pl.when(j == _SLOT_HG)
    def _():
        def fn(h, r):
            main_ref[0, 0, h] = _silu(r).astype(BF16)
        heads(fn)

    @pl.when(j == _N_GROUPS - 1)
    def _():
        def fn(h, r):
            gf_ref[0, h] = r
        heads(fn)


def _in_projection(x2, mod3, norm1_w, w_in, qnw, knw, tabs, bsz, seq):
    t, d = x2.shape
    tm = 1024
    rpb = seq // tm
    w_bf = w_in.astype(BF16)
    lanes = lambda g: jnp.broadcast_to(g[:, None], (HEAD_DIM, HEAD_DIM))
    qg = lanes(qnw * (HEAD_DIM ** -0.5 * LOG2E))
    kg = lanes(knw)

    def wcol(i, j):
        return (0, jnp.where(j <= 3, j, jnp.where(j == 4, 5, jnp.where(j == 5, 6, 4))))

    row = lambda i, j: (i, 0)
    const = lambda i, j: (0, 0)
    main_shape = jax.ShapeDtypeStruct((6, bsz, N_HEADS, seq, HEAD_DIM), BF16)
    gf_shape = jax.ShapeDtypeStruct((bsz, N_HEADS, seq, HEAD_DIM), F32)
    return pl.pallas_call(
        _inproj_kernel,
        grid=(t // tm, _N_GROUPS),
        in_specs=[pl.BlockSpec((tm, d), row),
                  pl.BlockSpec((1, 6, d), lambda i, j: (i // rpb, 0, 0)),
                  pl.BlockSpec((1, d), const),
                  pl.BlockSpec((d, GROUP_COLS), wcol),
                  pl.BlockSpec((HEAD_DIM, HEAD_DIM), const),
                  pl.BlockSpec((HEAD_DIM, HEAD_DIM), const),
                  pl.BlockSpec((ROPE_HALF, tm), lambda i, j: (0, i)),
                  pl.BlockSpec((ROPE_HALF, tm), lambda i, j: (0, i))],
        out_specs=[pl.BlockSpec((1, 1, N_HEADS, tm, HEAD_DIM),
                                lambda i, j: (jnp.minimum(j, 5), i // rpb, 0, i % rpb, 0)),
                   pl.BlockSpec((1, N_HEADS, tm, HEAD_DIM), lambda i, j: (i // rpb, 0, i % rpb, 0))],
        out_shape=[main_shape, gf_shape],
        scratch_shapes=[pltpu.VMEM((tm, d), BF16)],
        compiler_params=_params(("arbitrary", "arbitrary")),
        name="in_proj",
    )(x2, mod3, norm1_w.reshape(1, d), w_bf, qg, kg, *tabs)


def _perm_matrix(dil):
    per = PERM_ROWS // dil
    r = lax.broadcasted_iota(jnp.int32, (PERM_ROWS, PERM_ROWS), 0)
    s = lax.broadcasted_iota(jnp.int32, (PERM_ROWS, PERM_ROWS), 1)
    src = (r & (per - 1)) * dil + lax.shift_right_logical(r, int(math.log2(per)))
    return (s == src).astype(BF16)


def _band_finish(s, vv, bias):
    s = s + bias
    m = jnp.max(s, axis=-1, keepdims=True)
    p = jnp.exp2((s - m).astype(BF16))
    ones = jnp.ones((vv.shape[0], HEAD_DIM), BF16)
    ol = jnp.dot(p, jnp.concatenate([vv, ones], axis=1), preferred_element_type=F32)
    return ol[:, :HEAD_DIM], ol[:, HEAD_DIM:], m


def _run_tasks(tasks):
    for issue, finish in tasks:
        finish(issue())


def _attn_kernel(q_ref, k_ref, v_ref, o_ref, q4, k4, v4, q16, k16, v16, kp1, vp1, bias_scr, o_scr, l_scr,
                 m_scr):
    first = pl.program_id(2) == 0
    nblk = SPAN // ATTN_BLK
    ngrp = SPAN // PERM_ROWS

    @pl.when(first)
    def _():
        kp1[...] = jnp.zeros_like(kp1)
        vp1[...] = jnp.zeros_like(vp1)
        for buf in (k4, v4, k16, v16):
            buf[:, 0:ATTN_BLK, :] = jnp.zeros((buf.shape[0], ATTN_BLK, HEAD_DIM), BF16)

    qi = lax.broadcasted_iota(jnp.int32, (ATTN_BLK, 2 * ATTN_BLK), 0)
    kj = lax.broadcasted_iota(jnp.int32, (ATTN_BLK, 2 * ATTN_BLK), 1)
    dist = ATTN_BLK + qi - kj
    band = (dist >= 0) & (dist <= ATTN_BLK)
    bias_scr[0] = jnp.where(band, 0.0, -jnp.inf)
    first_key = jnp.where(first, ATTN_BLK, 0)
    bias_scr[1] = jnp.where(band & (kj >= first_key), 0.0, -jnp.inf)

    perm_tasks = []
    for dil, qd, kd, vd in ((DILATIONS[1], q4, k4, v4), (DILATIONS[2], q16, k16, v16)):
        perm = _perm_matrix(dil)
        per = PERM_ROWS // dil

        def scatter(dst, y, g, off, dil=dil, per=per):
            for c in range(dil):
                dst[c, off + g * per:off + (g + 1) * per, :] = y[c * per:(c + 1) * per, :]

        def rows(g):
            return slice(g * PERM_ROWS, (g + 1) * PERM_ROWS)

        def issue(a, ga, b, gb, perm=perm):
            x = jnp.concatenate([a[0, 0, 0, rows(ga), :], b[0, 0, 0, rows(gb), :]], axis=1)
            return jnp.dot(perm, x, preferred_element_type=F32)

        def finish(y, da, ga, db, gb, off, scatter=scatter):
            y = y.astype(BF16)
            scatter(da, y[:, :HEAD_DIM], ga, off)
            scatter(db, y[:, HEAD_DIM:], gb, off)

        for g in range(ngrp):
            perm_tasks.append((lambda g=g, issue=issue: issue(k_ref, g, v_ref, g),
                               lambda y, g=g, finish=finish, kd=kd, vd=vd: finish(y, kd, g, vd, g, ATTN_BLK)))
        for g in range(0, ngrp, 2):
            perm_tasks.append((lambda g=g, issue=issue: issue(q_ref, g, q_ref, g + 1),
                               lambda y, g=g, finish=finish, qd=qd: finish(y, qd, g, qd, g + 1, 0)))

    def attn_task(get_q, get_kk, get_vv, bias_idx, branch, start, stride):
        def finish(s):
            o, l, m = _band_finish(s, get_vv(), bias_scr[bias_idx])
            rws = pl.ds(start, ATTN_BLK, stride=stride) if stride > 1 else pl.ds(start, ATTN_BLK)
            o_scr[branch, rws, :] = o
            l_scr[branch, rws, :] = l
            m_scr[branch, rws, :] = jnp.broadcast_to(m, (ATTN_BLK, HEAD_DIM))
        return (lambda: _dot_nt(get_q(), get_kk()), finish)

    nat_tasks = []
    for r in range(nblk):
        lo = r * ATTN_BLK
        if r == 0:
            get_kk = lambda: jnp.concatenate([kp1[...], k_ref[0, 0, 0, 0:ATTN_BLK, :]], axis=0)
            get_vv = lambda: jnp.concatenate([vp1[...], v_ref[0, 0, 0, 0:ATTN_BLK, :]], axis=0)
        else:
            get_kk = lambda lo=lo: k_ref[0, 0, 0, lo - ATTN_BLK:lo + ATTN_BLK, :]
            get_vv = lambda lo=lo: v_ref[0, 0, 0, lo - ATTN_BLK:lo + ATTN_BLK, :]
        nat_tasks.append(attn_task(lambda lo=lo: q_ref[0, 0, 0, lo:lo + ATTN_BLK, :], get_kk, get_vv,
                                   1 if r == 0 else 0, 0, lo, 1))

    d4, d16 = DILATIONS[1], DILATIONS[2]
    d4_tasks = {}
    for c in range(d4):
        for a in range(nblk // d4):
            lo = a * ATTN_BLK
            d4_tasks[c, a] = attn_task(lambda c=c, lo=lo: q4[c, lo:lo + ATTN_BLK, :],
                                       lambda c=c, lo=lo: k4[c, lo:lo + 2 * ATTN_BLK, :],
                                       lambda c=c, lo=lo: v4[c, lo:lo + 2 * ATTN_BLK, :],
                                       1 if a == 0 else 0, 1, lo * d4 + c, d4)
    d16_tasks = [attn_task(lambda c=c: q16[c], lambda c=c: k16[c], lambda c=c: v16[c], 1, 2, c, d16)
                 for c in range(d16)]

    def merge_task(r):
        def finish(_):
            rows = slice(r * ATTN_BLK, (r + 1) * ATTN_BLK)
            m0, m1, m2 = m_scr[0, rows, :], m_scr[1, rows, :], m_scr[2, rows, :]
            mx = jnp.maximum(jnp.maximum(m0, m1), m2)
            w0, w1, w2 = jnp.exp2(m0 - mx), jnp.exp2(m1 - mx), jnp.exp2(m2 - mx)
            num = w0 * o_scr[0, rows, :] + w1 * o_scr[1, rows, :] + w2 * o_scr[2, rows, :]
            den = w0 * l_scr[0, rows, :] + w1 * l_scr[1, rows, :] + w2 * l_scr[2, rows, :]
            o_ref[0, rows, :] = (num * (1.0 / den)).astype(o_ref.dtype)
        return (lambda: None, finish)

    tasks = perm_tasks + d16_tasks
    per_q = nblk // d4
    for a in range(nblk // d4):
        tasks += [d4_tasks[c, a] for c in range(d4)] + nat_tasks[a * per_q:(a + 1) * per_q]
        tasks += [merge_task(r) for r in range(a * per_q, (a + 1) * per_q)]
    _run_tasks(tasks)

    kp1[...] = k_ref[0, 0, 0, SPAN - ATTN_BLK:SPAN, :]
    vp1[...] = v_ref[0, 0, 0, SPAN - ATTN_BLK:SPAN, :]
    for buf in (k4, v4, k16, v16):
        n = buf.shape[1]
        buf[:, 0:ATTN_BLK, :] = buf[:, n - ATTN_BLK:n, :]


def _dilated_attention(main, bsz, seq):
    nspan = seq // SPAN
    d4, d16 = DILATIONS[1], DILATIONS[2]
    blk = (1, 1, 1, SPAN, HEAD_DIM)
    spec = lambda slot: pl.BlockSpec(blk, lambda b, h, n, s=slot: (s, b, h, n, 0))
    cls = lambda dil, halo: pltpu.VMEM((dil, halo + SPAN // dil, HEAD_DIM), BF16)
    return pl.pallas_call(
        _attn_kernel,
        grid=(bsz, N_HEADS, nspan),
        in_specs=[spec(_SLOT_Q), spec(_SLOT_K), spec(_SLOT_V)],
        out_specs=pl.BlockSpec((1, SPAN, HEAD_DIM), lambda b, h, n: (b, n, h)),
        out_shape=jax.ShapeDtypeStruct((bsz, seq, GROUP_COLS), BF16),
        scratch_shapes=[cls(d4, 0), cls(d4, ATTN_BLK), cls(d4, ATTN_BLK),
                        cls(d16, 0), cls(d16, ATTN_BLK), cls(d16, ATTN_BLK),
                        pltpu.VMEM((ATTN_BLK, HEAD_DIM), BF16), pltpu.VMEM((ATTN_BLK, HEAD_DIM), BF16),
                        pltpu.VMEM((2, ATTN_BLK, 2 * ATTN_BLK), F32),
                        pltpu.VMEM((3, SPAN, HEAD_DIM), F32), pltpu.VMEM((3, SPAN, HEAD_DIM), F32),
                        pltpu.VMEM((3, SPAN, HEAD_DIM), F32)],
        compiler_params=_params(("arbitrary", "arbitrary", "arbitrary")),
        name="dilated_attn",
    )(main, main, main)


def _split3(x):
    hi = x.astype(BF16)
    r1 = x - hi.astype(F32)
    mid = r1.astype(BF16)
    lo = (r1 - mid.astype(F32)).astype(BF16)
    return hi, mid, lo


def _hgrn_chunks(qs, vs, gates, gs, lbs, nw, sts, consts, cs_scr, n_sub):
    tril3, xor_ts, causal, lane_bits, levels = consts
    c_rows = HG_CHUNK
    heads = range(len(qs))

    bs, css = [], []
    for h in heads:
        sg = 1.0 / (1.0 + jnp.exp(-gs[h]))
        f = lbs[h] + (1.0 - lbs[h]) * sg
        l2k = jnp.log2(1.0 - f)
        hi, mid, lo = _split3(jnp.log2(f))
        b = jnp.dot(tril3, jnp.concatenate([hi, mid, lo], axis=0), preferred_element_type=F32)
        bs.append(b)
        css.append(b - l2k)
        cs_scr[h] = css[h]
    b_lasts = [b[c_rows - 1:c_rows, :] for b in bs]

    a_mats = [None for _ in heads]
    for sz in levels:
        zero = jnp.zeros((sz, HEAD_DIM), F32)
        for h in heads:
            q, b, cs = qs[h], bs[h], css[h]
            qparts, kparts = [], []
            for p in range(c_rows // (2 * sz)):
                l0, r0, r1 = p * 2 * sz, p * 2 * sz + sz, (p + 1) * 2 * sz
                bref = b[r0 - 1:r0, :]
                kparts += [jnp.exp2(bref - cs[l0:r0, :]), zero]
                qparts += [zero, q[r0:r1, :] * jnp.exp2(b[r0:r1, :] - bref)]
            al = _dot_nt(jnp.concatenate(qparts, axis=0).astype(BF16),
                         jnp.concatenate(kparts, axis=0).astype(BF16))
            a_mats[h] = al if a_mats[h] is None else jnp.where(xor_ts < sz, a_mats[h], al)

    for h in heads:
        q, b, cs = qs[h], bs[h], css[h]
        blocks = []
        for i in range(c_rows // HG_SUB):
            rs = slice(i * HG_SUB, (i + 1) * HG_SUB)
            bb, qq = b[rs, :], q[rs, :]
            cols = []
            for s in range(HG_SUB):
                p = jnp.exp2(bb - cs_scr[h, i * HG_SUB + s:i * HG_SUB + s + 1, :]) * qq
                cols.append(jnp.broadcast_to(jnp.sum(p, axis=-1, keepdims=True), (HG_SUB, c_rows)))
            for bit in lane_bits:
                cols = [jnp.where(bit, cols[j + 1], cols[j]) for j in range(0, len(cols), 2)]
            blocks.append(cols[0])
        diag = jnp.where(causal, jnp.concatenate(blocks, axis=0), 0.0)
        a_mats[h] = jnp.where(xor_ts < HG_SUB, diag, a_mats[h]).astype(BF16)

    os_ = [jnp.dot(a_mats[h], vs[h], preferred_element_type=F32) for h in heads]

    upds = []
    for h in heads:
        kdec = jnp.exp2(b_lasts[h] - css[h]).astype(BF16)
        upds.append(jnp.dot(vs[h].astype(F32).T.astype(BF16), kdec, preferred_element_type=F32))

    sts = list(sts)
    for j in range(n_sub):
        for hd in range(len(sts)):
            u = hd * n_sub + j
            os_[u] = os_[u] + _dot_nt((qs[u] * jnp.exp2(bs[u])).astype(BF16), sts[hd].astype(BF16))
            sts[hd] = sts[hd] * jnp.exp2(b_lasts[u]) + upds[u]

    ys = [(_rms_rows(os_[h]) * nw * gates[h].astype(F32)).astype(BF16) for h in heads]
    return ys, sts


def _hgrn_kernel(q_ref, i_ref, g_ref, f_ref, lbl_ref, nw_ref, o_ref, st_scr, cs_scr):
    c_rows = HG_CHUNK
    nh = q_ref.shape[2]

    @pl.when(pl.program_id(2) == 0)
    def _():
        st_scr[...] = jnp.zeros_like(st_scr)

    lg = lbl_ref[...]
    e = jnp.exp(lg - jnp.max(lg, axis=0, keepdims=True))
    lb_all = e[0:1, :] / jnp.sum(e, axis=0, keepdims=True)

    ti = lax.broadcasted_iota(jnp.int32, (c_rows, c_rows), 0)
    si = lax.broadcasted_iota(jnp.int32, (c_rows, c_rows), 1)
    tril = (ti >= si).astype(BF16)
    levels = []
    sz = HG_SUB
    while sz < c_rows:
        levels.append(sz)
        sz *= 2
    lane = lax.broadcasted_iota(jnp.int32, (HG_SUB, c_rows), 1)
    lane_bits = tuple((lane & (1 << k)) != 0 for k in range(int(math.log2(HG_SUB))))
    consts = (jnp.concatenate([tril, tril, tril], axis=1), ti ^ si, ti >= si, lane_bits, tuple(levels))

    n_sub = HG_CHUNKS_PER_ITER

    def chunk(ci, carry):
        r0 = pl.multiple_of(ci * (n_sub * c_rows), n_sub * c_rows)
        units = [(h, pl.ds(r0 + j * c_rows, c_rows)) for h in range(nh) for j in range(n_sub)]
        ys, sts = _hgrn_chunks([q_ref[0, 0, h, rows, :].astype(F32) for h, rows in units],
                               [i_ref[0, 0, h, rows, :] for h, rows in units],
                               [g_ref[0, 0, h, rows, :] for h, rows in units],
                               [f_ref[0, h, rows, :] for h, rows in units],
                               [lb_all[:, h * HEAD_DIM:(h + 1) * HEAD_DIM] for h, _ in units],
                               nw_ref[...], [st_scr[h] for h in range(nh)], consts, cs_scr, n_sub)
        for h in range(nh):
            st_scr[h] = sts[h]
        for (h, rows), y in zip(units, ys):
            o_ref[0, rows, h * HEAD_DIM:(h + 1) * HEAD_DIM] = y
        return carry

    lax.fori_loop(0, o_ref.shape[1] // (n_sub * c_rows), chunk, 0)


def _hgrn2(main, gf, lb_logits, hg_norm_w, bsz, seq):
    tt = 2048
    nh = HG_HEADS_PER_STEP
    blk = (1, 1, nh, tt, HEAD_DIM)
    spec = lambda slot: pl.BlockSpec(blk, lambda b, h, n, s=slot: (s, b, h, n, 0))
    return pl.pallas_call(
        _hgrn_kernel,
        grid=(bsz, N_HEADS // nh, seq // tt),
        in_specs=[spec(_SLOT_HQ), spec(_SLOT_HI), spec(_SLOT_HG),
                  pl.BlockSpec((1, nh, tt, HEAD_DIM), lambda b, h, n: (b, h, n, 0)),
                  pl.BlockSpec((lb_logits.shape[0], nh * HEAD_DIM), lambda b, h, n: (0, h)),
                  pl.BlockSpec((1, HEAD_DIM), lambda b, h, n: (0, 0))],
        out_specs=pl.BlockSpec((1, tt, nh * HEAD_DIM), lambda b, h, n: (b, n, h)),
        out_shape=jax.ShapeDtypeStruct((bsz, seq, GROUP_COLS), BF16),
        scratch_shapes=[pltpu.VMEM((nh, HEAD_DIM, HEAD_DIM), F32),
                        pltpu.VMEM((nh * HG_CHUNKS_PER_ITER, HG_CHUNK, HEAD_DIM), F32)],
        compiler_params=_params(("arbitrary", "arbitrary", "arbitrary")),
        name="hgrn2",
    )(main, main, main, gf, lb_logits, hg_norm_w.reshape(1, HEAD_DIM))


def _outproj_kernel(x_ref, a_ref, g_ref, mod_ref, aw_ref, w_ref, o_ref):
    half = a_ref.shape[1]
    an = (_rms_rows(a_ref[...].astype(F32)) * aw_ref[...]).astype(BF16)
    mix = (jnp.dot(an, w_ref[0:half, :], preferred_element_type=F32)
           + jnp.dot(g_ref[...], w_ref[half:, :], preferred_element_type=F32))
    o_ref[...] = x_ref[...] + mod_ref[0, 2:3, :] * mix


def _out_projection(x2, attn2, hg2, mod3, attn_norm_w, w_out_bf, seq):
    t, d = x2.shape
    half = attn2.shape[1]
    tm = 512
    rpb = seq // tm
    row = lambda i: (i, 0)
    return pl.pallas_call(
        _outproj_kernel,
        grid=(t // tm,),
        in_specs=[pl.BlockSpec((tm, d), row),
                  pl.BlockSpec((tm, half), row),
                  pl.BlockSpec((tm, half), row),
                  pl.BlockSpec((1, 6, d), lambda i: (i // rpb, 0, 0)),
                  pl.BlockSpec((1, half), lambda i: (0, 0)),
                  pl.BlockSpec((2 * half, d), lambda i: (0, 0))],
        out_specs=pl.BlockSpec((tm, d), row),
        out_shape=jax.ShapeDtypeStruct((t, d), F32),
        compiler_params=_params(("arbitrary",)),
        name="out_proj",
    )(x2, attn2, hg2, mod3, attn_norm_w.reshape(1, half), w_out_bf)


def _mlp_kernel(x_ref, mod_ref, nw_ref, w1_ref, w2_ref, o_ref, h_scr, acc_scr):
    k = pl.program_id(1)

    @pl.when(k == 0)
    def _():
        _norm_modulate(x_ref, h_scr, nw_ref[...], mod_ref[0, 4:5, :], mod_ref[0, 3:4, :])

    def chunk(first):
        a = jnp.dot(h_scr[...], w1_ref[...], preferred_element_type=F32)
        a = jnp.square(jnp.maximum(a, 0.0)).astype(BF16)
        part = jnp.dot(a, w2_ref[...], preferred_element_type=F32)
        if first:
            acc_scr[...] = part
        else:
            acc_scr[...] += part

    pl.when(k == 0)(lambda: chunk(True))
    pl.when(k > 0)(lambda: chunk(False))

    @pl.when(k == pl.num_programs(1) - 1)
    def _():
        o_ref[...] = x_ref[...] + mod_ref[0, 5:6, :] * acc_scr[...]


def _mlp(x2, mod3, norm2_w, w1_bf, w2_bf, seq):
    t, d = x2.shape
    dff = w1_bf.shape[1]
    tm, tf = 512, 1024
    rpb = seq // tm
    row = lambda i, k: (i, 0)
    return pl.pallas_call(
        _mlp_kernel,
        grid=(t // tm, dff // tf),
        in_specs=[pl.BlockSpec((tm, d), row),
                  pl.BlockSpec((1, 6, d), lambda i, k: (i // rpb, 0, 0)),
                  pl.BlockSpec((1, d), lambda i, k: (0, 0)),
                  pl.BlockSpec((d, tf), lambda i, k: (0, k)),
                  pl.BlockSpec((tf, d), lambda i, k: (k, 0))],
        out_specs=pl.BlockSpec((tm, d), row),
        out_shape=jax.ShapeDtypeStruct((t, d), F32),
        scratch_shapes=[pltpu.VMEM((tm, d), BF16), pltpu.VMEM((tm, d), F32)],
        compiler_params=_params(("arbitrary", "arbitrary")),
        name="mlp",
    )(x2, mod3, norm2_w.reshape(1, d), w1_bf, w2_bf)


def kernel(x, c, positions, norm1_w, w_ada, b_ada, w_in, q_norm_w, k_norm_w, attn_out_norm_w,
           hg_lb_logits, hg_norm_w, w_out, norm2_w, w_ff1, w_ff2):
    bsz, seq, d = x.shape
    assert w_ada.shape[0] == 1, "single-layer block"
    assert seq % SPAN == 0 and d == 2 * GROUP_COLS and w_in.shape[2] == _N_GROUPS * GROUP_COLS
    x2 = x.reshape(bsz * seq, d)

    mod3 = _ada_mod(c, w_ada[0], b_ada[0]).reshape(bsz, 6, d)
    tabs = _rope_tables(positions)

    main, gf = _in_projection(x2, mod3, norm1_w[0], w_in[0], q_norm_w[0], k_norm_w[0], tabs, bsz, seq)
    attn = _dilated_attention(main, bsz, seq)
    hg = _hgrn2(main, gf, hg_lb_logits, hg_norm_w[0], bsz, seq)

    x1 = _out_projection(x2, attn.reshape(bsz * seq, GROUP_COLS), hg.reshape(bsz * seq, GROUP_COLS), mod3,
                         attn_out_norm_w[0], w_out[0].astype(BF16), seq)
    out = _mlp(x1, mod3, norm2_w[0], w_ff1[0].astype(BF16), w_ff2[0].astype(BF16), seq)
    return out.reshape(bsz, seq, d)
```

```python
import math

import jax
import jax.numpy as jnp
from jax import lax
from jax.experimental import pallas as pl
from jax.experimental.pallas import tpu as pltpu

F32 = jnp.float32
BF16 = jnp.bfloat16

HEAD_DIM = 128
N_HEADS = 8
GROUP_COLS = N_HEADS * HEAD_DIM
ROPE_THETA = 500000.0
ROPE_DIM = HEAD_DIM // 4
ROPE_HALF = ROPE_DIM // 2
DILATIONS = (1, 4, 16)
ATTN_BLK = 128
SPAN = ATTN_BLK * DILATIONS[-1]
PERM_ROWS = 256
D16_PITCH = 20
HG_CHUNK = 128
HG_SUB = 8
HG_HEADS_PER_STEP = 8
HG_CHUNKS_PER_ITER = 2
EPS = 1e-6
LOG2E = math.log2(math.e)
VMEM_LIMIT = 56 * 1024 * 1024

_N_GROUPS = 7
_SLOT_Q, _SLOT_K, _SLOT_V, _SLOT_HQ, _SLOT_HI, _SLOT_HG = range(6)


def _params(sem, vmem=VMEM_LIMIT):
    return pltpu.CompilerParams(dimension_semantics=sem, vmem_limit_bytes=vmem)


def _silu(x):
    return x * (0.5 * jnp.tanh(0.5 * x) + 0.5)


def _rms_rows(x, eps=EPS):
    return x * lax.rsqrt(jnp.mean(x * x, axis=-1, keepdims=True) + eps)


NORM_ROWS = 32


def _norm_modulate(x_ref, h_ref, gain, scale, shift):
    g = gain * (1.0 + scale)
    for c in range(x_ref.shape[0] // NORM_ROWS):
        rows = slice(c * NORM_ROWS, (c + 1) * NORM_ROWS)
        h_ref[rows, :] = (_rms_rows(x_ref[rows, :]) * g + shift).astype(BF16)


def _dot_nt(a, b):
    return lax.dot_general(a, b, (((1,), (1,)), ((), ())), preferred_element_type=F32)


def _ada_kernel(c_ref, w_ref, b_ref, o_ref):
    sc = _silu(c_ref[...]).astype(BF16)
    o_ref[...] = jnp.dot(sc, w_ref[...].astype(BF16), preferred_element_type=F32) + b_ref[...]


def _ada_mod(c, w_ada, b_ada):
    bsz, d = c.shape
    n = w_ada.shape[1]
    tn = 1024
    rows = 8
    c8 = jnp.pad(c, ((0, rows - bsz), (0, 0)))
    out = pl.pallas_call(
        _ada_kernel,
        grid=(n // tn,),
        in_specs=[pl.BlockSpec((rows, d), lambda j: (0, 0)),
                  pl.BlockSpec((d, tn), lambda j: (0, j)),
                  pl.BlockSpec((1, tn), lambda j: (0, j))],
        out_specs=pl.BlockSpec((rows, tn), lambda j: (0, j)),
        out_shape=jax.ShapeDtypeStruct((rows, n), F32),
        compiler_params=_params(("arbitrary",)),
        name="ada_mod",
    )(c8, w_ada, b_ada.reshape(1, n))
    return out[:bsz]


def _rope_kernel(pos_ref, invf_ref, cos_ref, sin_ref):
    ang = pos_ref[...] * invf_ref[...]
    cos_ref[...] = jnp.cos(ang)
    sin_ref[...] = jnp.sin(ang)


def _rope_tables(positions):
    t = positions.size
    ts = 2048
    inv_freq = ROPE_THETA ** (-(jnp.arange(ROPE_HALF, dtype=F32) * 2.0) / ROPE_DIM)
    invf = jnp.broadcast_to(inv_freq[:, None], (ROPE_HALF, ts))
    pos = positions.astype(F32).reshape(1, t)
    tab = jax.ShapeDtypeStruct((ROPE_HALF, t), F32)
    spec = pl.BlockSpec((ROPE_HALF, ts), lambda i: (0, i))
    return pl.pallas_call(
        _rope_kernel,
        grid=(t // ts,),
        in_specs=[pl.BlockSpec((1, ts), lambda i: (0, i)),
                  pl.BlockSpec((ROPE_HALF, ts), lambda i: (0, 0))],
        out_specs=[spec, spec],
        out_shape=[tab, tab],
        compiler_params=_params(("arbitrary",)),
        name="rope_tables",
    )(pos, invf)


_PROJ_COLS = 2 * HEAD_DIM
_QK_TOKENS = 256


def _inproj_kernel(x_ref, mod_ref, n1w_ref, w_ref, qg_ref, kg_ref, cos_ref, sin_ref,
                   main_ref, gf_ref, h_scr):
    j = pl.program_id(1)

    def heads(fn):
        for hp in range(GROUP_COLS // _PROJ_COLS):
            res = jnp.dot(h_scr[...], w_ref[:, hp * _PROJ_COLS:(hp + 1) * _PROJ_COLS],
                          preferred_element_type=F32)
            for hh in range(_PROJ_COLS // HEAD_DIM):
                fn(hp * (_PROJ_COLS // HEAD_DIM) + hh, res[:, hh * HEAD_DIM:(hh + 1) * HEAD_DIM])

    def qk_step(gain_ref, with_norm):
        if with_norm:
            _norm_modulate(x_ref, h_scr, n1w_ref[...], mod_ref[0, 1:2, :], mod_ref[0, 0:1, :])
        nt = _QK_TOKENS
        gain = jnp.concatenate([gain_ref[...]] * (nt // HEAD_DIM), axis=1)
        w_t = w_ref[...].T
        for part in range(h_scr.shape[0] // nt):
            toks = slice(part * nt, (part + 1) * nt)
            rt = _dot_nt(w_t, h_scr[toks, :])
            cos, sin = cos_ref[:, toks], sin_ref[:, toks]
            for h in range(N_HEADS):
                r = rt[h * HEAD_DIM:(h + 1) * HEAD_DIM, :]
                y = r * lax.rsqrt(jnp.mean(r * r, axis=0, keepdims=True) + EPS) * gain
                x1, x2 = y[0:ROPE_HALF, :], y[ROPE_HALF:ROPE_DIM, :]
                y = jnp.concatenate([x1 * cos - x2 * sin, x2 * cos + x1 * sin, y[ROPE_DIM:, :]], axis=0)
                main_ref[0, 0, h, toks, :] = y.T.astype(BF16)

    pl.when(j == _SLOT_Q)(lambda: qk_step(qg_ref, True))
    pl.when(j == _SLOT_K)(lambda: qk_step(kg_ref, False))

    @pl.when((j == _SLOT_V) | (j == _SLOT_HI))
    def _():
        def fn(h, r):
            main_ref[0, 0, h] = r.astype(BF16)
        heads(fn)

    @pl.when(j == _SLOT_HQ)
    def _():
        def fn(h, r):
            main_ref[0, 0, h] = (_silu(r) * (HEAD_DIM ** -0.5)).astype(BF16)
        heads(fn)

    @pl.when(j == _SLOT_HG)
    def _():
        def fn(h, r):
            main_ref[0, 0, h] = _silu(r).astype(BF16)
        heads(fn)

    @pl.when(j == _N_GROUPS - 1)
    def _():
        def fn(h, r):
            gf_ref[0, h] = r
        heads(fn)


def _in_projection(x2, mod3, norm1_w, w_in, qnw, knw, tabs, bsz, seq):
    t, d = x2.shape
    tm = 1024
    rpb = seq // tm
    w_bf = w_in.astype(BF16)
    lanes = lambda g: jnp.broadcast_to(g[:, None], (HEAD_DIM, HEAD_DIM))
    qg = lanes(qnw * (HEAD_DIM ** -0.5 * LOG2E))
    kg = lanes(knw)

    def wcol(i, j):
        return (0, jnp.where(j <= 3, j, jnp.where(j == 4, 5, jnp.where(j == 5, 6, 4))))

    row = lambda i, j: (i, 0)
    const = lambda i, j: (0, 0)
    main_shape = jax.ShapeDtypeStruct((6, bsz, N_HEADS, seq, HEAD_DIM), BF16)
    gf_shape = jax.ShapeDtypeStruct((bsz, N_HEADS, seq, HEAD_DIM), F32)
    return pl.pallas_call(
        _inproj_kernel,
        grid=(t // tm, _N_GROUPS),
        in_specs=[pl.BlockSpec((tm, d), row),
                  pl.BlockSpec((1, 6, d), lambda i, j: (i // rpb, 0, 0)),
                  pl.BlockSpec((1, d), const),
                  pl.BlockSpec((d, GROUP_COLS), wcol),
                  pl.BlockSpec((HEAD_DIM, HEAD_DIM), const),
                  pl.BlockSpec((HEAD_DIM, HEAD_DIM), const),
                  pl.BlockSpec((ROPE_HALF, tm), lambda i, j: (0, i)),
                  pl.BlockSpec((ROPE_HALF, tm), lambda i, j: (0, i))],
        out_specs=[pl.BlockSpec((1, 1, N_HEADS, tm, HEAD_DIM),
                                lambda i, j: (jnp.minimum(j, 5), i // rpb, 0, i % rpb, 0)),
                   pl.BlockSpec((1, N_HEADS, tm, HEAD_DIM), lambda i, j: (i // rpb, 0, i % rpb, 0))],
        out_shape=[main_shape, gf_shape],
        scratch_shapes=[pltpu.VMEM((tm, d), BF16)],
        compiler_params=_params(("arbitrary", "arbitrary")),
        name="in_proj",
    )(x2, mod3, norm1_w.reshape(1, d), w_bf, qg, kg, *tabs)


def _perm_matrix(dil):
    per = PERM_ROWS // dil
    r = lax.broadcasted_iota(jnp.int32, (PERM_ROWS, PERM_ROWS), 0)
    s = lax.broadcasted_iota(jnp.int32, (PERM_ROWS, PERM_ROWS), 1)
    src = (r & (per - 1)) * dil + lax.shift_right_logical(r, int(math.log2(per)))
    return (s == src).astype(BF16)


def _band_finish(s, vv, bias):
    s = s + bias
    m = jnp.max(s, axis=-1, keepdims=True)
    p = jnp.exp2((s - m).astype(BF16))
    ones = jnp.ones((vv.shape[0], HEAD_DIM), BF16)
    ol = jnp.dot(p, jnp.concatenate([vv, ones], axis=1), preferred_element_type=F32)
    return ol[:, :HEAD_DIM], ol[:, HEAD_DIM:], m


def _run_tasks(tasks):
    for issue, finish in tasks:
        finish(issue())


def _attn_kernel(q_ref, k_ref, v_ref, o_ref, q4, k4, v4, q16, k16, v16, kp1, vp1, bias_scr, o_scr, l_scr,
                 m_scr):
    first = pl.program_id(2) == 0
    nblk = SPAN // ATTN_BLK
    ngrp = SPAN // PERM_ROWS

    @pl.when(first)
    def _():
        kp1[...] = jnp.zeros_like(kp1)
        vp1[...] = jnp.zeros_like(vp1)
        for buf in (k4, v4, k16, v16):
            buf[:, 0:ATTN_BLK, :] = jnp.zeros((buf.shape[0], ATTN_BLK, HEAD_DIM), BF16)

    qi = lax.broadcasted_iota(jnp.int32, (ATTN_BLK, 2 * ATTN_BLK), 0)
    kj = lax.broadcasted_iota(jnp.int32, (ATTN_BLK, 2 * ATTN_BLK), 1)
    dist = ATTN_BLK + qi - kj
    band = (dist >= 0) & (dist <= ATTN_BLK)
    bias_scr[0] = jnp.where(band, 0.0, -jnp.inf)
    first_key = jnp.where(first, ATTN_BLK, 0)
    bias_scr[1] = jnp.where(band & (kj >= first_key), 0.0, -jnp.inf)

    perm_tasks = []
    for dil, qd, kd, vd in ((DILATIONS[1], q4, k4, v4), (DILATIONS[2], q16, k16, v16)):
        perm = _perm_matrix(dil)
        per = PERM_ROWS // dil

        def scatter(dst, y, g, off, dil=dil, per=per):
            for c in range(dil):
                dst[c, off + g * per:off + (g + 1) * per, :] = y[c * per:(c + 1) * per, :]

        def rows(g):
            return slice(g * PERM_ROWS, (g + 1) * PERM_ROWS)

        def issue(a, ga, b, gb, perm=perm):
            x = jnp.concatenate([a[0, 0, 0, rows(ga), :], b[0, 0, 0, rows(gb), :]], axis=1)
            return jnp.dot(perm, x, preferred_element_type=F32)

        def finish(y, da, ga, db, gb, off, scatter=scatter):
            y = y.astype(BF16)
            scatter(da, y[:, :HEAD_DIM], ga, off)
            scatter(db, y[:, HEAD_DIM:], gb, off)

        for g in range(ngrp):
            perm_tasks.append((lambda g=g, issue=issue: issue(k_ref, g, v_ref, g),
                               lambda y, g=g, finish=finish, kd=kd, vd=vd: finish(y, kd, g, vd, g, ATTN_BLK)))
        for g in range(0, ngrp, 2):
            perm_tasks.append((lambda g=g, issue=issue: issue(q_ref, g, q_ref, g + 1),
                               lambda y, g=g, finish=finish, qd=qd: finish(y, qd, g, qd, g + 1, 0)))

    def attn_task(get_q, get_kk, get_vv, bias_idx, branch, start, stride):
        def finish(s):
            o, l, m = _band_finish(s, get_vv(), bias_scr[bias_idx])
            if stride == DILATIONS[2]:
                rws = pl.ds(start, ATTN_BLK, stride=D16_PITCH)
            else:
                rws = pl.ds(start, ATTN_BLK, stride=stride) if stride > 1 else pl.ds(start, ATTN_BLK)
            o_scr[branch, rws, :] = o
            l_scr[branch, rws, :] = l
            m_scr[branch, rws, :] = jnp.broadcast_to(m, (ATTN_BLK, HEAD_DIM))
        return (lambda: _dot_nt(get_q(), get_kk()), finish)

    nat_tasks = []
    for r in range(nblk):
        lo = r * ATTN_BLK
        if r == 0:
            get_kk = lambda: jnp.concatenate([kp1[...], k_ref[0, 0, 0, 0:ATTN_BLK, :]], axis=0)
            get_vv = lambda: jnp.concatenate([vp1[...], v_ref[0, 0, 0, 0:ATTN_BLK, :]], axis=0)
        else:
            get_kk = lambda lo=lo: k_ref[0, 0, 0, lo - ATTN_BLK:lo + ATTN_BLK, :]
            get_vv = lambda lo=lo: v_ref[0, 0, 0, lo - ATTN_BLK:lo + ATTN_BLK, :]
        nat_tasks.append(attn_task(lambda lo=lo: q_ref[0, 0, 0, lo:lo + ATTN_BLK, :], get_kk, get_vv,
                                   1 if r == 0 else 0, 0, lo, 1))

    d4, d16 = DILATIONS[1], DILATIONS[2]
    d4_tasks = {}
    for c in range(d4):
        for a in range(nblk // d4):
            lo = a * ATTN_BLK
            d4_tasks[c, a] = attn_task(lambda c=c, lo=lo: q4[c, lo:lo + ATTN_BLK, :],
                                       lambda c=c, lo=lo: k4[c, lo:lo + 2 * ATTN_BLK, :],
                                       lambda c=c, lo=lo: v4[c, lo:lo + 2 * ATTN_BLK, :],
                                       1 if a == 0 else 0, 1, lo * d4 + c, d4)
    d16_tasks = [attn_task(lambda c=c: q16[c], lambda c=c: k16[c], lambda c=c: v16[c], 1, 2, c, d16)
                 for c in range(d16)]

    def merge_task(r):
        def finish(_):
            rows = slice(r * ATTN_BLK, (r + 1) * ATTN_BLK)

            def wide(scr):
                per = ATTN_BLK // d16
                return jnp.concatenate([scr[2, (r * per + k) * D16_PITCH:(r * per + k) * D16_PITCH + d16, :]
                                        for k in range(per)], axis=0)

            m0, m1, m2 = m_scr[0, rows, :], m_scr[1, rows, :], wide(m_scr)
            mx = jnp.maximum(jnp.maximum(m0, m1), m2)
            w0, w1, w2 = jnp.exp2(m0 - mx), jnp.exp2(m1 - mx), jnp.exp2(m2 - mx)
            num = w0 * o_scr[0, rows, :] + w1 * o_scr[1, rows, :] + w2 * wide(o_scr)
            den = w0 * l_scr[0, rows, :] + w1 * l_scr[1, rows, :] + w2 * wide(l_scr)
            o_ref[0, rows, :] = (num * (1.0 / den)).astype(o_ref.dtype)
        return (lambda: None, finish)

    tasks = perm_tasks + d16_tasks
    per_q = nblk // d4
    for a in range(nblk // d4):
        tasks += [d4_tasks[c, a] for c in range(d4)] + nat_tasks[a * per_q:(a + 1) * per_q]
        tasks += [merge_task(r) for r in range(a * per_q, (a + 1) * per_q)]
    _run_tasks(tasks)

    kp1[...] = k_ref[0, 0, 0, SPAN - ATTN_BLK:SPAN, :]
    vp1[...] = v_ref[0, 0, 0, SPAN - ATTN_BLK:SPAN, :]
    for buf in (k4, v4, k16, v16):
        n = buf.shape[1]
        buf[:, 0:ATTN_BLK, :] = buf[:, n - ATTN_BLK:n, :]


def _dilated_attention(main, bsz, seq):
    nspan = seq // SPAN
    d4, d16 = DILATIONS[1], DILATIONS[2]
    blk = (1, 1, 1, SPAN, HEAD_DIM)
    spec = lambda slot: pl.BlockSpec(blk, lambda b, h, n, s=slot: (s, b, h, n, 0))
    cls = lambda dil, halo: pltpu.VMEM((dil, halo + SPAN // dil, HEAD_DIM), BF16)
    return pl.pallas_call(
        _attn_kernel,
        grid=(bsz, N_HEADS, nspan),
        in_specs=[spec(_SLOT_Q), spec(_SLOT_K), spec(_SLOT_V)],
        out_specs=pl.BlockSpec((1, SPAN, HEAD_DIM), lambda b, h, n: (b, n, h)),
        out_shape=jax.ShapeDtypeStruct((bsz, seq, GROUP_COLS), BF16),
        scratch_shapes=[cls(d4, 0), cls(d4, ATTN_BLK), cls(d4, ATTN_BLK),
                        cls(d16, 0), cls(d16, ATTN_BLK), cls(d16, ATTN_BLK),
                        pltpu.VMEM((ATTN_BLK, HEAD_DIM), BF16), pltpu.VMEM((ATTN_BLK, HEAD_DIM), BF16),
                        pltpu.VMEM((2, ATTN_BLK, 2 * ATTN_BLK), F32),
                        ] + [pltpu.VMEM((3, SPAN // d16 * D16_PITCH, HEAD_DIM), F32)] * 3,
        compiler_params=_params(("arbitrary", "arbitrary", "arbitrary")),
        name="dilated_attn",
    )(main, main, main)


def _split3(x):
    hi = x.astype(BF16)
    r1 = x - hi.astype(F32)
    mid = r1.astype(BF16)
    lo = (r1 - mid.astype(F32)).astype(BF16)
    return hi, mid, lo


def _hgrn_chunks(qs, vs, gates, gs, lbs, nw, sts, consts, cs_scr, n_sub):
    tril3, xor_ts, causal, lane_bits, levels = consts
    c_rows = HG_CHUNK
    heads = range(len(qs))

    bs, css = [], []
    for h in heads:
        sg = 1.0 / (1.0 + jnp.exp(-gs[h]))
        f = lbs[h] + (1.0 - lbs[h]) * sg
        l2k = jnp.log2(1.0 - f)
        hi, mid, lo = _split3(jnp.log2(f))
        b = jnp.dot(tril3, jnp.concatenate([hi, mid, lo], axis=0), preferred_element_type=F32)
        bs.append(b)
        css.append(b - l2k)
        cs_scr[h] = css[h]
    b_lasts = [b[c_rows - 1:c_rows, :] for b in bs]

    a_mats = [None for _ in heads]
    for sz in levels:
        zero = jnp.zeros((sz, HEAD_DIM), F32)
        for h in heads:
            q, b, cs = qs[h], bs[h], css[h]
            qparts, kparts = [], []
            for p in range(c_rows // (2 * sz)):
                l0, r0, r1 = p * 2 * sz, p * 2 * sz + sz, (p + 1) * 2 * sz
                bref = b[r0 - 1:r0, :]
                kparts += [jnp.exp2(bref - cs[l0:r0, :]), zero]
                qparts += [zero, q[r0:r1, :] * jnp.exp2(b[r0:r1, :] - bref)]
            al = _dot_nt(jnp.concatenate(qparts, axis=0).astype(BF16),
                         jnp.concatenate(kparts, axis=0).astype(BF16))
            a_mats[h] = al if a_mats[h] is None else jnp.where(xor_ts < sz, a_mats[h], al)

    for h in heads:
        q, b, cs = qs[h], bs[h], css[h]
        blocks = []
        for i in range(c_rows // HG_SUB):
            rs = slice(i * HG_SUB, (i + 1) * HG_SUB)
            bb, qq = b[rs, :], q[rs, :]
            cols = []
            for s in range(HG_SUB):
                p = jnp.exp2(bb - cs_scr[h, i * HG_SUB + s:i * HG_SUB + s + 1, :]) * qq
                cols.append(jnp.broadcast_to(jnp.sum(p, axis=-1, keepdims=True), (HG_SUB, c_rows)))
            for bit in lane_bits:
                cols = [jnp.where(bit, cols[j + 1], cols[j]) for j in range(0, len(cols), 2)]
            blocks.append(cols[0])
        diag = jnp.where(causal, jnp.concatenate(blocks, axis=0), 0.0)
        a_mats[h] = jnp.where(xor_ts < HG_SUB, diag, a_mats[h]).astype(BF16)

    os_ = [jnp.dot(a_mats[h], vs[h], preferred_element_type=F32) for h in heads]

    upds = []
    for h in heads:
        kdec = jnp.exp2(b_lasts[h] - css[h]).astype(BF16)
        upds.append(jnp.dot(vs[h].astype(F32).T.astype(BF16), kdec, preferred_element_type=F32))

    sts = list(sts)
    for j in range(n_sub):
        for hd in range(len(sts)):
            u = hd * n_sub + j
            os_[u] = os_[u] + _dot_nt((qs[u] * jnp.exp2(bs[u])).astype(BF16), sts[hd].astype(BF16))
            sts[hd] = sts[hd] * jnp.exp2(b_lasts[u]) + upds[u]

    ys = [(_rms_rows(os_[h]) * nw * gates[h].astype(F32)).astype(BF16) for h in heads]
    return ys, sts


def _hgrn_kernel(q_ref, i_ref, g_ref, f_ref, lbl_ref, nw_ref, o_ref, st_scr, cs_scr):
    c_rows = HG_CHUNK
    nh = q_ref.shape[2]

    @pl.when(pl.program_id(2) == 0)
    def _():
        st_scr[...] = jnp.zeros_like(st_scr)

    lg = lbl_ref[...]
    e = jnp.exp(lg - jnp.max(lg, axis=0, keepdims=True))
    lb_all = e[0:1, :] / jnp.sum(e, axis=0, keepdims=True)

    ti = lax.broadcasted_iota(jnp.int32, (c_rows, c_rows), 0)
    si = lax.broadcasted_iota(jnp.int32, (c_rows, c_rows), 1)
    tril = (ti >= si).astype(BF16)
    levels = []
    sz = HG_SUB
    while sz < c_rows:
        levels.append(sz)
        sz *= 2
    lane = lax.broadcasted_iota(jnp.int32, (HG_SUB, c_rows), 1)
    lane_bits = tuple((lane & (1 << k)) != 0 for k in range(int(math.log2(HG_SUB))))
    consts = (jnp.concatenate([tril, tril, tril], axis=1), ti ^ si, ti >= si, lane_bits, tuple(levels))

    n_sub = HG_CHUNKS_PER_ITER

    def chunk(ci, carry):
        r0 = pl.multiple_of(ci * (n_sub * c_rows), n_sub * c_rows)
        units = [(h, pl.ds(r0 + j * c_rows, c_rows)) for h in range(nh) for j in range(n_sub)]
        ys, sts = _hgrn_chunks([q_ref[0, 0, h, rows, :].astype(F32) for h, rows in units],
                               [i_ref[0, 0, h, rows, :] for h, rows in units],
                               [g_ref[0, 0, h, rows, :] for h, rows in units],
                               [f_ref[0, h, rows, :] for h, rows in units],
                               [lb_all[:, h * HEAD_DIM:(h + 1) * HEAD_DIM] for h, _ in units],
                               nw_ref[...], [st_scr[h] for h in range(nh)], consts, cs_scr, n_sub)
        for h in range(nh):
            st_scr[h] = sts[h]
        for (h, rows), y in zip(units, ys):
            o_ref[0, rows, h * HEAD_DIM:(h + 1) * HEAD_DIM] = y
        return carry

    lax.fori_loop(0, o_ref.shape[1] // (n_sub * c_rows), chunk, 0)


def _hgrn2(main, gf, lb_logits, hg_norm_w, bsz, seq):
    tt = 2048
    nh = HG_HEADS_PER_STEP
    blk = (1, 1, nh, tt, HEAD_DIM)
    spec = lambda slot: pl.BlockSpec(blk, lambda b, h, n, s=slot: (s, b, h, n, 0))
    return pl.pallas_call(
        _hgrn_kernel,
        grid=(bsz, N_HEADS // nh, seq // tt),
        in_specs=[spec(_SLOT_HQ), spec(_SLOT_HI), spec(_SLOT_HG),
                  pl.BlockSpec((1, nh, tt, HEAD_DIM), lambda b, h, n: (b, h, n, 0)),
                  pl.BlockSpec((lb_logits.shape[0], nh * HEAD_DIM), lambda b, h, n: (0, h)),
                  pl.BlockSpec((1, HEAD_DIM), lambda b, h, n: (0, 0))],
        out_specs=pl.BlockSpec((1, tt, nh * HEAD_DIM), lambda b, h, n: (b, n, h)),
        out_shape=jax.ShapeDtypeStruct((bsz, seq, GROUP_COLS), BF16),
        scratch_shapes=[pltpu.VMEM((nh, HEAD_DIM, HEAD_DIM), F32),
                        pltpu.VMEM((nh * HG_CHUNKS_PER_ITER, HG_CHUNK, HEAD_DIM), F32)],
        compiler_params=_params(("arbitrary", "arbitrary", "arbitrary")),
        name="hgrn2",
    )(main, main, main, gf, lb_logits, hg_norm_w.reshape(1, HEAD_DIM))


def _outproj_kernel(x_ref, a_ref, g_ref, mod_ref, aw_ref, w_ref, o_ref):
    half = a_ref.shape[1]
    an = (_rms_rows(a_ref[...].astype(F32)) * aw_ref[...]).astype(BF16)
    mix = (jnp.dot(an, w_ref[0:half, :], preferred_element_type=F32)
           + jnp.dot(g_ref[...], w_ref[half:, :], preferred_element_type=F32))
    o_ref[...] = x_ref[...] + mod_ref[0, 2:3, :] * mix


def _out_projection(x2, attn2, hg2, mod3, attn_norm_w, w_out_bf, seq):
    t, d = x2.shape
    half = attn2.shape[1]
    tm = 512
    rpb = seq // tm
    row = lambda i: (i, 0)
    return pl.pallas_call(
        _outproj_kernel,
        grid=(t // tm,),
        in_specs=[pl.BlockSpec((tm, d), row),
                  pl.BlockSpec((tm, half), row),
                  pl.BlockSpec((tm, half), row),
                  pl.BlockSpec((1, 6, d), lambda i: (i // rpb, 0, 0)),
                  pl.BlockSpec((1, half), lambda i: (0, 0)),
                  pl.BlockSpec((2 * half, d), lambda i: (0, 0))],
        out_specs=pl.BlockSpec((tm, d), row),
        out_shape=jax.ShapeDtypeStruct((t, d), F32),
        compiler_params=_params(("arbitrary",)),
        name="out_proj",
    )(x2, attn2, hg2, mod3, attn_norm_w.reshape(1, half), w_out_bf)


def _mlp_kernel(x_ref, mod_ref, nw_ref, w1_ref, w2_ref, o_ref, h_scr, acc_scr):
    k = pl.program_id(1)

    @pl.when(k == 0)
    def _():
        _norm_modulate(x_ref, h_scr, nw_ref[...], mod_ref[0, 4:5, :], mod_ref[0, 3:4, :])

    def chunk(first):
        a = jnp.dot(h_scr[...], w1_ref[...], preferred_element_type=F32)
        a = jnp.square(jnp.maximum(a, 0.0)).astype(BF16)
        part = jnp.dot(a, w2_ref[...], preferred_element_type=F32)
        if first:
            acc_scr[...] = part
        else:
            acc_scr[...] += part

    pl.when(k == 0)(lambda: chunk(True))
    pl.when(k > 0)(lambda: chunk(False))

    @pl.when(k == pl.num_programs(1) - 1)
    def _():
        o_ref[...] = x_ref[...] + mod_ref[0, 5:6, :] * acc_scr[...]


def _mlp(x2, mod3, norm2_w, w1_bf, w2_bf, seq):
    t, d = x2.shape
    dff = w1_bf.shape[1]
    tm, tf = 512, 1024
    rpb = seq // tm
    row = lambda i, k: (i, 0)
    return pl.pallas_call(
        _mlp_kernel,
        grid=(t // tm, dff // tf),
        in_specs=[pl.BlockSpec((tm, d), row),
                  pl.BlockSpec((1, 6, d), lambda i, k: (i // rpb, 0, 0)),
                  pl.BlockSpec((1, d), lambda i, k: (0, 0)),
                  pl.BlockSpec((d, tf), lambda i, k: (0, k)),
                  pl.BlockSpec((tf, d), lambda i, k: (k, 0))],
        out_specs=pl.BlockSpec((tm, d), row),
        out_shape=jax.ShapeDtypeStruct((t, d), F32),
        scratch_shapes=[pltpu.VMEM((tm, d), BF16), pltpu.VMEM((tm, d), F32)],
        compiler_params=_params(("arbitrary", "arbitrary")),
        name="mlp",
    )(x2, mod3, norm2_w.reshape(1, d), w1_bf, w2_bf)


def kernel(x, c, positions, norm1_w, w_ada, b_ada, w_in, q_norm_w, k_norm_w, attn_out_norm_w,
           hg_lb_logits, hg_norm_w, w_out, norm2_w, w_ff1, w_ff2):
    bsz, seq, d = x.shape
    assert w_ada.shape[0] == 1, "single-layer block"
    assert seq % SPAN == 0 and d == 2 * GROUP_COLS and w_in.shape[2] == _N_GROUPS * GROUP_COLS
    x2 = x.reshape(bsz * seq, d)

    mod3 = _ada_mod(c, w_ada[0], b_ada[0]).reshape(bsz, 6, d)
    tabs = _rope_tables(positions)

    main, gf = _in_projection(x2, mod3, norm1_w[0], w_in[0], q_norm_w[0], k_norm_w[0], tabs, bsz, seq)
    attn = _dilated_attention(main, bsz, seq)
    hg = _hgrn2(main, gf, hg_lb_logits, hg_norm_w[0], bsz, seq)

    x1 = _out_projection(x2, attn.reshape(bsz * seq, GROUP_COLS), hg.reshape(bsz * seq, GROUP_COLS), mod3,
                         attn_out_norm_w[0], w_out[0].astype(BF16), seq)
    out = _mlp(x1, mod3, norm2_w[0], w_ff1[0].astype(BF16), w_ff2[0].astype(BF16), seq)
    return out.reshape(bsz, seq, d)
```

```python
import math

import jax
import jax.numpy as jnp
from jax import lax
from jax.experimental import pallas as pl
from jax.experimental.pallas import tpu as pltpu

F32 = jnp.float32
BF16 = jnp.bfloat16

HEAD_DIM = 128
N_HEADS = 8
GROUP_COLS = N_HEADS * HEAD_DIM
ROPE_THETA = 500000.0
ROPE_DIM = HEAD_DIM // 4
ROPE_HALF = ROPE_DIM // 2
DILATIONS = (1, 4, 16)
ATTN_BLK = 128
SPAN = ATTN_BLK * DILATIONS[-1]
PERM_ROWS = 256
D16_PITCH = 20
HG_CHUNK = 128
HG_SUB = 8
HG_HEADS_PER_STEP = 8
HG_CHUNKS_PER_ITER = 2
EPS = 1e-6
LOG2E = math.log2(math.e)
VMEM_LIMIT = 56 * 1024 * 1024

_N_GROUPS = 7
_SLOT_Q, _SLOT_K, _SLOT_V, _SLOT_HQ, _SLOT_HI, _SLOT_HG = range(6)


def _params(sem, vmem=VMEM_LIMIT):
    return pltpu.CompilerParams(dimension_semantics=sem, vmem_limit_bytes=vmem)


def _silu(x):
    return x * (0.5 * jnp.tanh(0.5 * x) + 0.5)


def _rms_rows(x, eps=EPS):
    return x * lax.rsqrt(jnp.mean(x * x, axis=-1, keepdims=True) + eps)


NORM_ROWS = 32


def _norm_modulate(x_ref, h_ref, gain, scale, shift):
    g = gain * (1.0 + scale)
    for c in range(x_ref.shape[0] // NORM_ROWS):
        rows = slice(c * NORM_ROWS, (c + 1) * NORM_ROWS)
        h_ref[rows, :] = (_rms_rows(x_ref[rows, :]) * g + shift).astype(BF16)


def _dot_nt(a, b):
    return lax.dot_general(a, b, (((1,), (1,)), ((), ())), preferred_element_type=F32)


def _ada_kernel(c_ref, w_ref, b_ref, o_ref):
    sc = _silu(c_ref[...]).astype(BF16)
    o_ref[...] = jnp.dot(sc, w_ref[...].astype(BF16), preferred_element_type=F32) + b_ref[...]


def _ada_mod(c, w_ada, b_ada):
    bsz, d = c.shape
    n = w_ada.shape[1]
    tn = 1024
    rows = 8
    c8 = jnp.pad(c, ((0, rows - bsz), (0, 0)))
    out = pl.pallas_call(
        _ada_kernel,
        grid=(n // tn,),
        in_specs=[pl.BlockSpec((rows, d), lambda j: (0, 0)),
                  pl.BlockSpec((d, tn), lambda j: (0, j)),
                  pl.BlockSpec((1, tn), lambda j: (0, j))],
        out_specs=pl.BlockSpec((rows, tn), lambda j: (0, j)),
        out_shape=jax.ShapeDtypeStruct((rows, n), F32),
        compiler_params=_params(("arbitrary",)),
        name="ada_mod",
    )(c8, w_ada, b_ada.reshape(1, n))
    return out[:bsz]


def _rope_kernel(pos_ref, invf_ref, cos_ref, sin_ref):
    ang = pos_ref[...] * invf_ref[...]
    cos_ref[...] = jnp.cos(ang)
    sin_ref[...] = jnp.sin(ang)


def _rope_tables(positions):
    t = positions.size
    ts = 2048
    inv_freq = ROPE_THETA ** (-(jnp.arange(ROPE_HALF, dtype=F32) * 2.0) / ROPE_DIM)
    invf = jnp.broadcast_to(inv_freq[:, None], (ROPE_HALF, ts))
    pos = positions.astype(F32).reshape(1, t)
    tab = jax.ShapeDtypeStruct((ROPE_HALF, t), F32)
    spec = pl.BlockSpec((ROPE_HALF, ts), lambda i: (0, i))
    return pl.pallas_call(
        _rope_kernel,
        grid=(t // ts,),
        in_specs=[pl.BlockSpec((1, ts), lambda i: (0, i)),
                  pl.BlockSpec((ROPE_HALF, ts), lambda i: (0, 0))],
        out_specs=[spec, spec],
        out_shape=[tab, tab],
        compiler_params=_params(("arbitrary",)),
        name="rope_tables",
    )(pos, invf)


_PROJ_COLS = 2 * HEAD_DIM
_QK_TOKENS = 256


def _inproj_kernel(x_ref, mod_ref, n1w_ref, w_ref, qg_ref, kg_ref, cos_ref, sin_ref,
                   main_ref, gf_ref, h_scr):
    j = pl.program_id(1)

    def heads(fn):
        for hp in range(GROUP_COLS // _PROJ_COLS):
            res = jnp.dot(h_scr[...], w_ref[:, hp * _PROJ_COLS:(hp + 1) * _PROJ_COLS],
                          preferred_element_type=F32)
            for hh in range(_PROJ_COLS // HEAD_DIM):
                fn(hp * (_PROJ_COLS // HEAD_DIM) + hh, res[:, hh * HEAD_DIM:(hh + 1) * HEAD_DIM])

    def qk_step(gain_ref, with_norm):
        if with_norm:
            _norm_modulate(x_ref, h_scr, n1w_ref[...], mod_ref[0, 1:2, :], mod_ref[0, 0:1, :])
        nt = _QK_TOKENS
        gain = jnp.concatenate([gain_ref[...]] * (nt // HEAD_DIM), axis=1)
        w_t = w_ref[...].T
        for part in range(h_scr.shape[0] // nt):
            toks = slice(part * nt, (part + 1) * nt)
            rt = _dot_nt(w_t, h_scr[toks, :])
            cos, sin = cos_ref[:, toks], sin_ref[:, toks]
            for h in range(N_HEADS):
                r = rt[h * HEAD_DIM:(h + 1) * HEAD_DIM, :]
                y = r * lax.rsqrt(jnp.mean(r * r, axis=0, keepdims=True) + EPS) * gain
                x1, x2 = y[0:ROPE_HALF, :], y[ROPE_HALF:ROPE_DIM, :]
                y = jnp.concatenate([x1 * cos - x2 * sin, x2 * cos + x1 * sin, y[ROPE_DIM:, :]], axis=0)
                main_ref[0, 0, h, toks, :] = y.T.astype(BF16)

    pl.when(j == _SLOT_Q)(lambda: qk_step(qg_ref, True))
    pl.when(j == _SLOT_K)(lambda: qk_step(kg_ref, False))

    @pl.when((j == _SLOT_V) | (j == _SLOT_HI))
    def _():
        def fn(h, r):
            main_ref[0, 0, h] = r.astype(BF16)
        heads(fn)

    @pl.when(j == _SLOT_HQ)
    def _():
        def fn(h, r):
            main_ref[0, 0, h] = (_silu(r) * (HEAD_DIM ** -0.5)).astype(BF16)
        heads(fn)

    @pl.when(j == _SLOT_HG)
    def _():
        def fn(h, r):
            main_ref[0, 0, h] = _silu(r).astype(BF16)
        heads(fn)

    @pl.when(j == _N_GROUPS - 1)
    def _():
        def fn(h, r):
            gf_ref[0, h] = r
        heads(fn)


def _in_projection(x2, mod3, norm1_w, w_in, qnw, knw, tabs, bsz, seq):
    t, d = x2.shape
    tm = 1024
    rpb = seq // tm
    w_bf = w_in.astype(BF16)
    lanes = lambda g: jnp.broadcast_to(g[:, None], (HEAD_DIM, HEAD_DIM))
    qg = lanes(qnw * (HEAD_DIM ** -0.5 * LOG2E))
    kg = lanes(knw)

    def wcol(i, j):
        return (0, jnp.where(j <= 3, j, jnp.where(j == 4, 5, jnp.where(j == 5, 6, 4))))

    row = lambda i, j: (i, 0)
    const = lambda i, j: (0, 0)
    main_shape = jax.ShapeDtypeStruct((6, bsz, N_HEADS, seq, HEAD_DIM), BF16)
    gf_shape = jax.ShapeDtypeStruct((bsz, N_HEADS, seq, HEAD_DIM), F32)
    return pl.pallas_call(
        _inproj_kernel,
        grid=(t // tm, _N_GROUPS),
        in_specs=[pl.BlockSpec((tm, d), row),
                  pl.BlockSpec((1, 6, d), lambda i, j: (i // rpb, 0, 0)),
                  pl.BlockSpec((1, d), const),
                  pl.BlockSpec((d, GROUP_COLS), wcol),
                  pl.BlockSpec((HEAD_DIM, HEAD_DIM), const),
                  pl.BlockSpec((HEAD_DIM, HEAD_DIM), const),
                  pl.BlockSpec((ROPE_HALF, tm), lambda i, j: (0, i)),
                  pl.BlockSpec((ROPE_HALF, tm), lambda i, j: (0, i))],
        out_specs=[pl.BlockSpec((1, 1, N_HEADS, tm, HEAD_DIM),
                                lambda i, j: (jnp.minimum(j, 5), i // rpb, 0, i % rpb, 0)),
                   pl.BlockSpec((1, N_HEADS, tm, HEAD_DIM), lambda i, j: (i // rpb, 0, i % rpb, 0))],
        out_shape=[main_shape, gf_shape],
        scratch_shapes=[pltpu.VMEM((tm, d), BF16)],
        compiler_params=_params(("arbitrary", "arbitrary")),
        name="in_proj",
    )(x2, mod3, norm1_w.reshape(1, d), w_bf, qg, kg, *tabs)


def _perm_matrix(dil):
    per = PERM_ROWS // dil
    r = lax.broadcasted_iota(jnp.int32, (PERM_ROWS, PERM_ROWS), 0)
    s = lax.broadcasted_iota(jnp.int32, (PERM_ROWS, PERM_ROWS), 1)
    src = (r & (per - 1)) * dil + lax.shift_right_logical(r, int(math.log2(per)))
    return (s == src).astype(BF16)


def _band_finish(s, vv, bias):
    s = s + bias
    m = jnp.max(s, axis=-1, keepdims=True)
    p = jnp.exp2((s - m).astype(BF16))
    ones = jnp.ones((vv.shape[0], HEAD_DIM), BF16)
    ol = jnp.dot(p, jnp.concatenate([vv, ones], axis=1), preferred_element_type=F32)
    return ol[:, :HEAD_DIM], ol[:, HEAD_DIM:], m


def _run_tasks(tasks):
    for issue, finish in tasks:
        finish(issue())


def _attn_kernel(q_ref, k_ref, v_ref, o_ref, q4, k4, v4, q16, k16, v16, kp1, vp1, bias_scr, o_scr, l_scr,
                 m_scr, qf_scr, kf_scr, vf_scr):
    first = pl.program_id(2) == 0
    nblk = SPAN // ATTN_BLK
    ngrp = SPAN // PERM_ROWS

    @pl.when(first)
    def _():
        kp1[...] = jnp.zeros_like(kp1)
        vp1[...] = jnp.zeros_like(vp1)
        for buf in (k4, v4, k16, v16):
            buf[:, 0:ATTN_BLK, :] = jnp.zeros((buf.shape[0], ATTN_BLK, HEAD_DIM), BF16)

    qi = lax.broadcasted_iota(jnp.int32, (ATTN_BLK, 2 * ATTN_BLK), 0)
    kj = lax.broadcasted_iota(jnp.int32, (ATTN_BLK, 2 * ATTN_BLK), 1)
    dist = ATTN_BLK + qi - kj
    band = (dist >= 0) & (dist <= ATTN_BLK)
    bias_scr[0] = jnp.where(band, 0.0, -jnp.inf)
    first_key = jnp.where(first, ATTN_BLK, 0)
    bias_scr[1] = jnp.where(band & (kj >= first_key), 0.0, -jnp.inf)

    perm_tasks = []

    def gather4(src, dst, off, buf):
        def finish(_):
            buf[...] = src[0, 0, 0].astype(F32)
            for c in range(DILATIONS[1]):
                dst[c, off:off + SPAN // DILATIONS[1], :] = (
                    buf[pl.ds(c, SPAN // DILATIONS[1], stride=DILATIONS[1]), :].astype(BF16))
        return (lambda: None, finish)

    perm_tasks += [gather4(k_ref, k4, ATTN_BLK, kf_scr), gather4(v_ref, v4, ATTN_BLK, vf_scr),
                   gather4(q_ref, q4, 0, qf_scr)]

    for dil, qd, kd, vd in ((DILATIONS[2], q16, k16, v16),):
        perm = _perm_matrix(dil)
        per = PERM_ROWS // dil

        def scatter(dst, y, g, off, dil=dil, per=per):
            for c in range(dil):
                dst[c, off + g * per:off + (g + 1) * per, :] = y[c * per:(c + 1) * per, :]

        def rows(g):
            return slice(g * PERM_ROWS, (g + 1) * PERM_ROWS)

        def issue(a, ga, b, gb, perm=perm):
            x = jnp.concatenate([a[0, 0, 0, rows(ga), :], b[0, 0, 0, rows(gb), :]], axis=1)
            return jnp.dot(perm, x, preferred_element_type=F32)

        def finish(y, da, ga, db, gb, off, scatter=scatter):
            y = y.astype(BF16)
            scatter(da, y[:, :HEAD_DIM], ga, off)
            scatter(db, y[:, HEAD_DIM:], gb, off)

        for g in range(ngrp):
            perm_tasks.append((lambda g=g, issue=issue: issue(k_ref, g, v_ref, g),
                               lambda y, g=g, finish=finish, kd=kd, vd=vd: finish(y, kd, g, vd, g, ATTN_BLK)))
        for g in range(0, ngrp, 2):
            perm_tasks.append((lambda g=g, issue=issue: issue(q_ref, g, q_ref, g + 1),
                               lambda y, g=g, finish=finish, qd=qd: finish(y, qd, g, qd, g + 1, 0)))

    def attn_task(get_q, get_kk, get_vv, bias_idx, branch, start, stride):
        def finish(s):
            o, l, m = _band_finish(s, get_vv(), bias_scr[bias_idx])
            if stride == DILATIONS[2]:
                rws = pl.ds(start, ATTN_BLK, stride=D16_PITCH)
            else:
                rws = pl.ds(start, ATTN_BLK, stride=stride) if stride > 1 else pl.ds(start, ATTN_BLK)
            o_scr[branch, rws, :] = o
            l_scr[branch, rws, :] = l
            m_scr[branch, rws, :] = jnp.broadcast_to(m, (ATTN_BLK, HEAD_DIM))
        return (lambda: _dot_nt(get_q(), get_kk()), finish)

    nat_tasks = []
    for r in range(nblk):
        lo = r * ATTN_BLK
        if r == 0:
            get_kk = lambda: jnp.concatenate([kp1[...], k_ref[0, 0, 0, 0:ATTN_BLK, :]], axis=0)
            get_vv = lambda: jnp.concatenate([vp1[...], v_ref[0, 0, 0, 0:ATTN_BLK, :]], axis=0)
        else:
            get_kk = lambda lo=lo: k_ref[0, 0, 0, lo - ATTN_BLK:lo + ATTN_BLK, :]
            get_vv = lambda lo=lo: v_ref[0, 0, 0, lo - ATTN_BLK:lo + ATTN_BLK, :]
        nat_tasks.append(attn_task(lambda lo=lo: q_ref[0, 0, 0, lo:lo + ATTN_BLK, :], get_kk, get_vv,
                                   1 if r == 0 else 0, 0, lo, 1))

    d4, d16 = DILATIONS[1], DILATIONS[2]
    d4_tasks = {}
    for c in range(d4):
        for a in range(nblk // d4):
            lo = a * ATTN_BLK
            d4_tasks[c, a] = attn_task(lambda c=c, lo=lo: q4[c, lo:lo + ATTN_BLK, :],
                                       lambda c=c, lo=lo: k4[c, lo:lo + 2 * ATTN_BLK, :],
                                       lambda c=c, lo=lo: v4[c, lo:lo + 2 * ATTN_BLK, :],
                                       1 if a == 0 else 0, 1, lo * d4 + c, d4)
    d16_tasks = [attn_task(lambda c=c: q16[c], lambda c=c: k16[c], lambda c=c: v16[c], 1, 2, c, d16)
                 for c in range(d16)]

    def merge_task(r):
        def finish(_):
            rows = slice(r * ATTN_BLK, (r + 1) * ATTN_BLK)

            def wide(scr):
                per = ATTN_BLK // d16
                return jnp.concatenate([scr[2, (r * per + k) * D16_PITCH:(r * per + k) * D16_PITCH + d16, :]
                                        for k in range(per)], axis=0)

            m0, m1, m2 = m_scr[0, rows, :], m_scr[1, rows, :], wide(m_scr)
            mx = jnp.maximum(jnp.maximum(m0, m1), m2)
            w0, w1, w2 = jnp.exp2(m0 - mx), jnp.exp2(m1 - mx), jnp.exp2(m2 - mx)
            num = w0 * o_scr[0, rows, :] + w1 * o_scr[1, rows, :] + w2 * wide(o_scr)
            den = w0 * l_scr[0, rows, :] + w1 * l_scr[1, rows, :] + w2 * wide(l_scr)
            o_ref[0, rows, :] = (num * (1.0 / den)).astype(o_ref.dtype)
        return (lambda: None, finish)

    tasks = perm_tasks + d16_tasks
    per_q = nblk // d4
    for a in range(nblk // d4):
        tasks += [d4_tasks[c, a] for c in range(d4)] + nat_tasks[a * per_q:(a + 1) * per_q]
        tasks += [merge_task(r) for r in range(a * per_q, (a + 1) * per_q)]
    _run_tasks(tasks)

    kp1[...] = k_ref[0, 0, 0, SPAN - ATTN_BLK:SPAN, :]
    vp1[...] = v_ref[0, 0, 0, SPAN - ATTN_BLK:SPAN, :]
    for buf in (k4, v4, k16, v16):
        n = buf.shape[1]
        buf[:, 0:ATTN_BLK, :] = buf[:, n - ATTN_BLK:n, :]


def _dilated_attention(main, bsz, seq):
    nspan = seq // SPAN
    d4, d16 = DILATIONS[1], DILATIONS[2]
    blk = (1, 1, 1, SPAN, HEAD_DIM)
    spec = lambda slot: pl.BlockSpec(blk, lambda b, h, n, s=slot: (s, b, h, n, 0))
    cls = lambda dil, halo: pltpu.VMEM((dil, halo + SPAN // dil, HEAD_DIM), BF16)
    return pl.pallas_call(
        _attn_kernel,
        grid=(bsz, N_HEADS, nspan),
        in_specs=[spec(_SLOT_Q), spec(_SLOT_K), spec(_SLOT_V)],
        out_specs=pl.BlockSpec((1, SPAN, HEAD_DIM), lambda b, h, n: (b, n, h)),
        out_shape=jax.ShapeDtypeStruct((bsz, seq, GROUP_COLS), BF16),
        scratch_shapes=[cls(d4, 0), cls(d4, ATTN_BLK), cls(d4, ATTN_BLK),
                        cls(d16, 0), cls(d16, ATTN_BLK), cls(d16, ATTN_BLK),
                        pltpu.VMEM((ATTN_BLK, HEAD_DIM), BF16), pltpu.VMEM((ATTN_BLK, HEAD_DIM), BF16),
                        pltpu.VMEM((2, ATTN_BLK, 2 * ATTN_BLK), F32),
                        ] + [pltpu.VMEM((3, SPAN // d16 * D16_PITCH, HEAD_DIM), F32)] * 3
                        + [pltpu.VMEM((SPAN, HEAD_DIM), F32)] * 3,
        compiler_params=_params(("arbitrary", "arbitrary", "arbitrary")),
        name="dilated_attn",
    )(main, main, main)


def _split3(x):
    hi = x.astype(BF16)
    r1 = x - hi.astype(F32)
    mid = r1.astype(BF16)
    lo = (r1 - mid.astype(F32)).astype(BF16)
    return hi, mid, lo


def _hgrn_chunks(qs, vs, gates, gs, lbs, nw, sts, consts, cs_scr, n_sub):
    tril3, xor_ts, causal, lane_bits, levels = consts
    c_rows = HG_CHUNK
    heads = range(len(qs))

    bs, css = [], []
    for h in heads:
        sg = 1.0 / (1.0 + jnp.exp(-gs[h]))
        f = lbs[h] + (1.0 - lbs[h]) * sg
        l2k = jnp.log2(1.0 - f)
        hi, mid, lo = _split3(jnp.log2(f))
        b = jnp.dot(tril3, jnp.concatenate([hi, mid, lo], axis=0), preferred_element_type=F32)
        bs.append(b)
        css.append(b - l2k)
        cs_scr[h] = css[h]
    b_lasts = [b[c_rows - 1:c_rows, :] for b in bs]

    a_mats = [None for _ in heads]
    for sz in levels:
        zero = jnp.zeros((sz, HEAD_DIM), F32)
        for h in heads:
            q, b, cs = qs[h], bs[h], css[h]
            qparts, kparts = [], []
            for p in range(c_rows // (2 * sz)):
                l0, r0, r1 = p * 2 * sz, p * 2 * sz + sz, (p + 1) * 2 * sz
                bref = b[r0 - 1:r0, :]
                kparts += [jnp.exp2(bref - cs[l0:r0, :]), zero]
                qparts += [zero, q[r0:r1, :] * jnp.exp2(b[r0:r1, :] - bref)]
            al = _dot_nt(jnp.concatenate(qparts, axis=0).astype(BF16),
                         jnp.concatenate(kparts, axis=0).astype(BF16))
            a_mats[h] = al if a_mats[h] is None else jnp.where(xor_ts < sz, a_mats[h], al)

    for h in heads:
        q, b, cs = qs[h], bs[h], css[h]
        blocks = []
        for i in range(c_rows // HG_SUB):
            rs = slice(i * HG_SUB, (i + 1) * HG_SUB)
            bb, qq = b[rs, :], q[rs, :]
            cols = []
            for s in range(HG_SUB):
                p = jnp.exp2(bb - cs_scr[h, i * HG_SUB + s:i * HG_SUB + s + 1, :]) * qq
                cols.append(jnp.broadcast_to(jnp.sum(p, axis=-1, keepdims=True), (HG_SUB, c_rows)))
            for bit in lane_bits:
                cols = [jnp.where(bit, cols[j + 1], cols[j]) for j in range(0, len(cols), 2)]
            blocks.append(cols[0])
        diag = jnp.where(causal, jnp.concatenate(blocks, axis=0), 0.0)
        a_mats[h] = jnp.where(xor_ts < HG_SUB, diag, a_mats[h]).astype(BF16)

    os_ = [jnp.dot(a_mats[h], vs[h], preferred_element_type=F32) for h in heads]

    upds = []
    for h in heads:
        kdec = jnp.exp2(b_lasts[h] - css[h]).astype(BF16)
        upds.append(jnp.dot(vs[h].astype(F32).T.astype(BF16), kdec, preferred_element_type=F32))

    sts = list(sts)
    for j in range(n_sub):
        for hd in range(len(sts)):
            u = hd * n_sub + j
            os_[u] = os_[u] + _dot_nt((qs[u] * jnp.exp2(bs[u])).astype(BF16), sts[hd].astype(BF16))
            sts[hd] = sts[hd] * jnp.exp2(b_lasts[u]) + upds[u]

    ys = [(_rms_rows(os_[h]) * nw * gates[h].astype(F32)).astype(BF16) for h in heads]
    return ys, sts


def _hgrn_kernel(q_ref, i_ref, g_ref, f_ref, lbl_ref, nw_ref, o_ref, st_scr, cs_scr):
    c_rows = HG_CHUNK
    nh = q_ref.shape[2]

    @pl.when(pl.program_id(2) == 0)
    def _():
        st_scr[...] = jnp.zeros_like(st_scr)

    lg = lbl_ref[...]
    e = jnp.exp(lg - jnp.max(lg, axis=0, keepdims=True))
    lb_all = e[0:1, :] / jnp.sum(e, axis=0, keepdims=True)

    ti = lax.broadcasted_iota(jnp.int32, (c_rows, c_rows), 0)
    si = lax.broadcasted_iota(jnp.int32, (c_rows, c_rows), 1)
    tril = (ti >= si).astype(BF16)
    levels = []
    sz = HG_SUB
    while sz < c_rows:
        levels.append(sz)
        sz *= 2
    lane = lax.broadcasted_iota(jnp.int32, (HG_SUB, c_rows), 1)
    lane_bits = tuple((lane & (1 << k)) != 0 for k in range(int(math.log2(HG_SUB))))
    consts = (jnp.concatenate([tril, tril, tril], axis=1), ti ^ si, ti >= si, lane_bits, tuple(levels))

    n_sub = HG_CHUNKS_PER_ITER

    def chunk(ci, carry):
        r0 = pl.multiple_of(ci * (n_sub * c_rows), n_sub * c_rows)
        units = [(h, pl.ds(r0 + j * c_rows, c_rows)) for h in range(nh) for j in range(n_sub)]
        ys, sts = _hgrn_chunks([q_ref[0, 0, h, rows, :].astype(F32) for h, rows in units],
                               [i_ref[0, 0, h, rows, :] for h, rows in units],
                               [g_ref[0, 0, h, rows, :] for h, rows in units],
                               [f_ref[0, h, rows, :] for h, rows in units],
                               [lb_all[:, h * HEAD_DIM:(h + 1) * HEAD_DIM] for h, _ in units],
                               nw_ref[...], [st_scr[h] for h in range(nh)], consts, cs_scr, n_sub)
        for h in range(nh):
            st_scr[h] = sts[h]
        for (h, rows), y in zip(units, ys):
            o_ref[0, rows, h * HEAD_DIM:(h + 1) * HEAD_DIM] = y
        return carry

    lax.fori_loop(0, o_ref.shape[1] // (n_sub * c_rows), chunk, 0)


def _hgrn2(main, gf, lb_logits, hg_norm_w, bsz, seq):
    tt = 2048
    nh = HG_HEADS_PER_STEP
    blk = (1, 1, nh, tt, HEAD_DIM)
    spec = lambda slot: pl.BlockSpec(blk, lambda b, h, n, s=slot: (s, b, h, n, 0))
    return pl.pallas_call(
        _hgrn_kernel,
        grid=(bsz, N_HEADS // nh, seq // tt),
        in_specs=[spec(_SLOT_HQ), spec(_SLOT_HI), spec(_SLOT_HG),
                  pl.BlockSpec((1, nh, tt, HEAD_DIM), lambda b, h, n: (b, h, n, 0)),
                  pl.BlockSpec((lb_logits.shape[0], nh * HEAD_DIM), lambda b, h, n: (0, h)),
                  pl.BlockSpec((1, HEAD_DIM), lambda b, h, n: (0, 0))],
        out_specs=pl.BlockSpec((1, tt, nh * HEAD_DIM), lambda b, h, n: (b, n, h)),
        out_shape=jax.ShapeDtypeStruct((bsz, seq, GROUP_COLS), BF16),
        scratch_shapes=[pltpu.VMEM((nh, HEAD_DIM, HEAD_DIM), F32),
                        pltpu.VMEM((nh * HG_CHUNKS_PER_ITER, HG_CHUNK, HEAD_DIM), F32)],
        compiler_params=_params(("arbitrary", "arbitrary", "arbitrary")),
        name="hgrn2",
    )(main, main, main, gf, lb_logits, hg_norm_w.reshape(1, HEAD_DIM))


def _outproj_kernel(x_ref, a_ref, g_ref, mod_ref, aw_ref, w_ref, o_ref):
    half = a_ref.shape[1]
    an = (_rms_rows(a_ref[...].astype(F32)) * aw_ref[...]).astype(BF16)
    mix = (jnp.dot(an, w_ref[0:half, :], preferred_element_type=F32)
           + jnp.dot(g_ref[...], w_ref[half:, :], preferred_element_type=F32))
    o_ref[...] = x_ref[...] + mod_ref[0, 2:3, :] * mix


def _out_projection(x2, attn2, hg2, mod3, attn_norm_w, w_out_bf, seq):
    t, d = x2.shape
    half = attn2.shape[1]
    tm = 512
    rpb = seq // tm
    row = lambda i: (i, 0)
    return pl.pallas_call(
        _outproj_kernel,
        grid=(t // tm,),
        in_specs=[pl.BlockSpec((tm, d), row),
                  pl.BlockSpec((tm, half), row),
                  pl.BlockSpec((tm, half), row),
                  pl.BlockSpec((1, 6, d), lambda i: (i // rpb, 0, 0)),
                  pl.BlockSpec((1, half), lambda i: (0, 0)),
                  pl.BlockSpec((2 * half, d), lambda i: (0, 0))],
        out_specs=pl.BlockSpec((tm, d), row),
        out_shape=jax.ShapeDtypeStruct((t, d), F32),
        compiler_params=_params(("arbitrary",)),
        name="out_proj",
    )(x2, attn2, hg2, mod3, attn_norm_w.reshape(1, half), w_out_bf)


def _mlp_kernel(x_ref, mod_ref, nw_ref, w1_ref, w2_ref, o_ref, h_scr, acc_scr):
    k = pl.program_id(1)

    @pl.when(k == 0)
    def _():
        _norm_modulate(x_ref, h_scr, nw_ref[...], mod_ref[0, 4:5, :], mod_ref[0, 3:4, :])

    last = pl.num_programs(1) - 1

    def chunk(first, final):
        a = jnp.dot(h_scr[...], w1_ref[...], preferred_element_type=F32)
        a = jnp.square(jnp.maximum(a, 0.0)).astype(BF16)
        part = jnp.dot(a, w2_ref[...], preferred_element_type=F32)
        if first:
            acc_scr[...] = part
        elif final:
            o_ref[...] = x_ref[...] + mod_ref[0, 5:6, :] * (acc_scr[...] + part)
        else:
            acc_scr[...] += part

    pl.when(k == 0)(lambda: chunk(True, False))
    pl.when((k > 0) & (k < last))(lambda: chunk(False, False))
    pl.when(k == last)(lambda: chunk(False, True))


def _mlp(x2, mod3, norm2_w, w1_bf, w2_bf, seq):
    t, d = x2.shape
    dff = w1_bf.shape[1]
    tm, tf = 512, 1024
    rpb = seq // tm
    row = lambda i, k: (i, 0)
    return pl.pallas_call(
        _mlp_kernel,
        grid=(t // tm, dff // tf),
        in_specs=[pl.BlockSpec((tm, d), row),
                  pl.BlockSpec((1, 6, d), lambda i, k: (i // rpb, 0, 0)),
                  pl.BlockSpec((1, d), lambda i, k: (0, 0)),
                  pl.BlockSpec((d, tf), lambda i, k: (0, k)),
                  pl.BlockSpec((tf, d), lambda i, k: (k, 0))],
        out_specs=pl.BlockSpec((tm, d), row),
        out_shape=jax.ShapeDtypeStruct((t, d), F32),
        scratch_shapes=[pltpu.VMEM((tm, d), BF16), pltpu.VMEM((tm, d), F32)],
        compiler_params=_params(("arbitrary", "arbitrary")),
        name="mlp",
    )(x2, mod3, norm2_w.reshape(1, d), w1_bf, w2_bf)


def kernel(x, c, positions, norm1_w, w_ada, b_ada, w_in, q_norm_w, k_norm_w, attn_out_norm_w,
           hg_lb_logits, hg_norm_w, w_out, norm2_w, w_ff1, w_ff2):
    bsz, seq, d = x.shape
    assert w_ada.shape[0] == 1, "single-layer block"
    assert seq % SPAN == 0 and d == 2 * GROUP_COLS and w_in.shape[2] == _N_GROUPS * GROUP_COLS
    x2 = x.reshape(bsz * seq, d)

    mod3 = _ada_mod(c, w_ada[0], b_ada[0]).reshape(bsz, 6, d)
    tabs = _rope_tables(positions)

    main, gf = _in_projection(x2, mod3, norm1_w[0], w_in[0], q_norm_w[0], k_norm_w[0], tabs, bsz, seq)
    attn = _dilated_attention(main, bsz, seq)
    hg = _hgrn2(main, gf, hg_lb_logits, hg_norm_w[0], bsz, seq)

    x1 = _out_projection(x2, attn.reshape(bsz * seq, GROUP_COLS), hg.reshape(bsz * seq, GROUP_COLS), mod3,
                         attn_out_norm_w[0], w_out[0].astype(BF16), seq)
    out = _mlp(x1, mod3, norm2_w[0], w_ff1[0].astype(BF16), w_ff2[0].astype(BF16), seq)
    return out.reshape(bsz, seq, d)
```

```python
import math

import jax
import jax.numpy as jnp
from jax import lax
from jax.experimental import pallas as pl
from jax.experimental.pallas import tpu as pltpu

F32 = jnp.float32
BF16 = jnp.bfloat16

HEAD_DIM = 128
N_HEADS = 8
GROUP_COLS = N_HEADS * HEAD_DIM
ROPE_THETA = 500000.0
ROPE_DIM = HEAD_DIM // 4
ROPE_HALF = ROPE_DIM // 2
DILATIONS = (1, 4, 16)
ATTN_BLK = 128
SPAN = ATTN_BLK * DILATIONS[-1]
PERM_ROWS = 256
D16_PITCH = 20
HG_CHUNK = 128
HG_SUB = 8
HG_HEADS_PER_STEP = 8
HG_CHUNKS_PER_ITER = 2
EPS = 1e-6
LOG2E = math.log2(math.e)

ADA_COLS = 1024
ROPE_TOKENS = 2048
INPROJ_ROWS = 1024
OUTPROJ_ROWS = 512
MLP_ROWS, MLP_FF_COLS = 512, 1024
HG_TOKENS = 2048
VMEM_LIMIT = 56 * 1024 * 1024

_N_GROUPS = 7
_SLOT_Q, _SLOT_K, _SLOT_V, _SLOT_HQ, _SLOT_HI, _SLOT_HG = range(6)


def _params(sem, vmem=VMEM_LIMIT):
    return pltpu.CompilerParams(dimension_semantics=sem, vmem_limit_bytes=vmem)


def _silu(x):
    return x * (0.5 * jnp.tanh(0.5 * x) + 0.5)


def _rms_rows(x, eps=EPS):
    return x * lax.rsqrt(jnp.mean(x * x, axis=-1, keepdims=True) + eps)


NORM_ROWS = 32


def _norm_modulate(x_ref, h_ref, gain, scale, shift):
    g = gain * (1.0 + scale)
    for c in range(x_ref.shape[0] // NORM_ROWS):
        rows = slice(c * NORM_ROWS, (c + 1) * NORM_ROWS)
        h_ref[rows, :] = (_rms_rows(x_ref[rows, :]) * g + shift).astype(BF16)


def _dot_nt(a, b):
    return lax.dot_general(a, b, (((1,), (1,)), ((), ())), preferred_element_type=F32)


def _ada_kernel(c_ref, w_ref, b_ref, o_ref):
    sc = _silu(c_ref[...]).astype(BF16)
    o_ref[...] = jnp.dot(sc, w_ref[...].astype(BF16), preferred_element_type=F32) + b_ref[...]


def _ada_mod(c, w_ada, b_ada):
    bsz, d = c.shape
    n = w_ada.shape[1]
    tn = ADA_COLS
    rows = 8
    c8 = jnp.pad(c, ((0, rows - bsz), (0, 0)))
    out = pl.pallas_call(
        _ada_kernel,
        grid=(n // tn,),
        in_specs=[pl.BlockSpec((rows, d), lambda j: (0, 0)),
                  pl.BlockSpec((d, tn), lambda j: (0, j)),
                  pl.BlockSpec((1, tn), lambda j: (0, j))],
        out_specs=pl.BlockSpec((rows, tn), lambda j: (0, j)),
        out_shape=jax.ShapeDtypeStruct((rows, n), F32),
        compiler_params=_params(("arbitrary",)),
        name="ada_mod",
    )(c8, w_ada, b_ada.reshape(1, n))
    return out[:bsz]


def _rope_kernel(pos_ref, invf_ref, cos_ref, sin_ref):
    ang = pos_ref[...] * invf_ref[...]
    cos_ref[...] = jnp.cos(ang)
    sin_ref[...] = jnp.sin(ang)


def _rope_tables(positions):
    t = positions.size
    ts = ROPE_TOKENS
    inv_freq = ROPE_THETA ** (-(jnp.arange(ROPE_HALF, dtype=F32) * 2.0) / ROPE_DIM)
    invf = jnp.broadcast_to(inv_freq[:, None], (ROPE_HALF, ts))
    pos = positions.astype(F32).reshape(1, t)
    tab = jax.ShapeDtypeStruct((ROPE_HALF, t), F32)
    spec = pl.BlockSpec((ROPE_HALF, ts), lambda i: (0, i))
    return pl.pallas_call(
        _rope_kernel,
        grid=(t // ts,),
        in_specs=[pl.BlockSpec((1, ts), lambda i: (0, i)),
                  pl.BlockSpec((ROPE_HALF, ts), lambda i: (0, 0))],
        out_specs=[spec, spec],
        out_shape=[tab, tab],
        compiler_params=_params(("arbitrary",)),
        name="rope_tables",
    )(pos, invf)


_PROJ_COLS = 2 * HEAD_DIM
_QK_TOKENS = 256


def _inproj_kernel(x_ref, mod_ref, n1w_ref, w_ref, qg_ref, kg_ref, cos_ref, sin_ref,
                   main_ref, gf_ref, h_scr):
    j = pl.program_id(1)

    def heads(fn):
        for hp in range(GROUP_COLS // _PROJ_COLS):
            res = jnp.dot(h_scr[...], w_ref[:, hp * _PROJ_COLS:(hp + 1) * _PROJ_COLS],
                          preferred_element_type=F32)
            for hh in range(_PROJ_COLS // HEAD_DIM):
                fn(hp * (_PROJ_COLS // HEAD_DIM) + hh, res[:, hh * HEAD_DIM:(hh + 1) * HEAD_DIM])

    def qk_step(gain_ref, with_norm):
        if with_norm:
            _norm_modulate(x_ref, h_scr, n1w_ref[...], mod_ref[0, 1:2, :], mod_ref[0, 0:1, :])
        nt = _QK_TOKENS
        gain = jnp.concatenate([gain_ref[...]] * (nt // HEAD_DIM), axis=1)
        w_t = w_ref[...].T
        for part in range(h_scr.shape[0] // nt):
            toks = slice(part * nt, (part + 1) * nt)
            rt = _dot_nt(w_t, h_scr[toks, :])
            cos, sin = cos_ref[:, toks], sin_ref[:, toks]
            for h in range(N_HEADS):
                r = rt[h * HEAD_DIM:(h + 1) * HEAD_DIM, :]
                y = r * lax.rsqrt(jnp.mean(r * r, axis=0, keepdims=True) + EPS) * gain
                x1, x2 = y[0:ROPE_HALF, :], y[ROPE_HALF:ROPE_DIM, :]
                y = jnp.concatenate([x1 * cos - x2 * sin, x2 * cos + x1 * sin, y[ROPE_DIM:, :]], axis=0)
                main_ref[0, 0, h, toks, :] = y.T.astype(BF16)

    pl.when(j == _SLOT_Q)(lambda: qk_step(qg_ref, True))
    pl.when(j == _SLOT_K)(lambda: qk_step(kg_ref, False))

    @pl.when((j == _SLOT_V) | (j == _SLOT_HI))
    def _():
        def fn(h, r):
            main_ref[0, 0, h] = r.astype(BF16)
        heads(fn)

    @pl.when(j == _SLOT_HQ)
    def _():
        def fn(h, r):
            main_ref[0, 0, h] = (_silu(r) * (HEAD_DIM ** -0.5)).astype(BF16)
        heads(fn)

    @pl.when(j == _SLOT_HG)
    def _():
        def fn(h, r):
            main_ref[0, 0, h] = _silu(r).astype(BF16)
        heads(fn)

    @pl.when(j == _N_GROUPS - 1)
    def _():
        def fn(h, r):
            gf_ref[0, h] = r
        heads(fn)


def _in_projection(x2, mod3, norm1_w, w_in, qnw, knw, tabs, bsz, seq):
    t, d = x2.shape
    tm = INPROJ_ROWS
    rpb = seq // tm
    w_bf = w_in.astype(BF16)
    lanes = lambda g: jnp.broadcast_to(g[:, None], (HEAD_DIM, HEAD_DIM))
    qg = lanes(qnw * (HEAD_DIM ** -0.5 * LOG2E))
    kg = lanes(knw)

    def wcol(i, j):
        return (0, jnp.where(j <= 3, j, jnp.where(j == 4, 5, jnp.where(j == 5, 6, 4))))

    row = lambda i, j: (i, 0)
    const = lambda i, j: (0, 0)
    main_shape = jax.ShapeDtypeStruct((6, bsz, N_HEADS, seq, HEAD_DIM), BF16)
    gf_shape = jax.ShapeDtypeStruct((bsz, N_HEADS, seq, HEAD_DIM), F32)
    return pl.pallas_call(
        _inproj_kernel,
        grid=(t // tm, _N_GROUPS),
        in_specs=[pl.BlockSpec((tm, d), row),
                  pl.BlockSpec((1, 6, d), lambda i, j: (i // rpb, 0, 0)),
                  pl.BlockSpec((1, d), const),
                  pl.BlockSpec((d, GROUP_COLS), wcol),
                  pl.BlockSpec((HEAD_DIM, HEAD_DIM), const),
                  pl.BlockSpec((HEAD_DIM, HEAD_DIM), const),
                  pl.BlockSpec((ROPE_HALF, tm), lambda i, j: (0, i)),
                  pl.BlockSpec((ROPE_HALF, tm), lambda i, j: (0, i))],
        out_specs=[pl.BlockSpec((1, 1, N_HEADS, tm, HEAD_DIM),
                                lambda i, j: (jnp.minimum(j, 5), i // rpb, 0, i % rpb, 0)),
                   pl.BlockSpec((1, N_HEADS, tm, HEAD_DIM), lambda i, j: (i // rpb, 0, i % rpb, 0))],
        out_shape=[main_shape, gf_shape],
        scratch_shapes=[pltpu.VMEM((tm, d), BF16)],
        compiler_params=_params(("arbitrary", "arbitrary")),
        name="in_proj",
    )(x2, mod3, norm1_w.reshape(1, d), w_bf, qg, kg, *tabs)


def _perm_matrix(dil):
    per = PERM_ROWS // dil
    r = lax.broadcasted_iota(jnp.int32, (PERM_ROWS, PERM_ROWS), 0)
    s = lax.broadcasted_iota(jnp.int32, (PERM_ROWS, PERM_ROWS), 1)
    src = (r & (per - 1)) * dil + lax.shift_right_logical(r, int(math.log2(per)))
    return (s == src).astype(BF16)


def _band_finish(s, vv, bias):
    s = s + bias
    m = jnp.max(s, axis=-1, keepdims=True)
    p = jnp.exp2((s - m).astype(BF16))
    ones = jnp.ones((vv.shape[0], HEAD_DIM), BF16)
    ol = jnp.dot(p, jnp.concatenate([vv, ones], axis=1), preferred_element_type=F32)
    return ol[:, :HEAD_DIM], ol[:, HEAD_DIM:], m


def _run_tasks(tasks):
    for issue, finish in tasks:
        finish(issue())


def _attn_kernel(q_ref, k_ref, v_ref, o_ref, q4, k4, v4, q16, k16, v16, kp1, vp1, bias_scr, o_scr, l_scr,
                 m_scr, qf_scr, kf_scr, vf_scr):
    first = pl.program_id(2) == 0
    nblk = SPAN // ATTN_BLK
    ngrp = SPAN // PERM_ROWS

    @pl.when(first)
    def _():
        kp1[...] = jnp.zeros_like(kp1)
        vp1[...] = jnp.zeros_like(vp1)
        for buf in (k4, v4, k16, v16):
            buf[:, 0:ATTN_BLK, :] = jnp.zeros((buf.shape[0], ATTN_BLK, HEAD_DIM), BF16)

    qi = lax.broadcasted_iota(jnp.int32, (ATTN_BLK, 2 * ATTN_BLK), 0)
    kj = lax.broadcasted_iota(jnp.int32, (ATTN_BLK, 2 * ATTN_BLK), 1)
    dist = ATTN_BLK + qi - kj
    band = (dist >= 0) & (dist <= ATTN_BLK)
    bias_scr[0] = jnp.where(band, 0.0, -jnp.inf)
    first_key = jnp.where(first, ATTN_BLK, 0)
    bias_scr[1] = jnp.where(band & (kj >= first_key), 0.0, -jnp.inf)

    perm_tasks = []

    def gather4(src, dst, off, buf):
        def finish(_):
            buf[...] = src[0, 0, 0].astype(F32)
            for c in range(DILATIONS[1]):
                dst[c, off:off + SPAN // DILATIONS[1], :] = (
                    buf[pl.ds(c, SPAN // DILATIONS[1], stride=DILATIONS[1]), :].astype(BF16))
        return (lambda: None, finish)

    perm_tasks += [gather4(k_ref, k4, ATTN_BLK, kf_scr), gather4(v_ref, v4, ATTN_BLK, vf_scr),
                   gather4(q_ref, q4, 0, qf_scr)]

    for dil, qd, kd, vd in ((DILATIONS[2], q16, k16, v16),):
        perm = _perm_matrix(dil)
        per = PERM_ROWS // dil

        def scatter(dst, y, g, off, dil=dil, per=per):
            for c in range(dil):
                dst[c, off + g * per:off + (g + 1) * per, :] = y[c * per:(c + 1) * per, :]

        def rows(g):
            return slice(g * PERM_ROWS, (g + 1) * PERM_ROWS)

        def issue(a, ga, b, gb, perm=perm):
            x = jnp.concatenate([a[0, 0, 0, rows(ga), :], b[0, 0, 0, rows(gb), :]], axis=1)
            return jnp.dot(perm, x, preferred_element_type=F32)

        def finish(y, da, ga, db, gb, off, scatter=scatter):
            y = y.astype(BF16)
            scatter(da, y[:, :HEAD_DIM], ga, off)
            scatter(db, y[:, HEAD_DIM:], gb, off)

        for g in range(ngrp):
            perm_tasks.append((lambda g=g, issue=issue: issue(k_ref, g, v_ref, g),
                               lambda y, g=g, finish=finish, kd=kd, vd=vd: finish(y, kd, g, vd, g, ATTN_BLK)))
        for g in range(0, ngrp, 2):
            perm_tasks.append((lambda g=g, issue=issue: issue(q_ref, g, q_ref, g + 1),
                               lambda y, g=g, finish=finish, qd=qd: finish(y, qd, g, qd, g + 1, 0)))

    def attn_task(get_q, get_kk, get_vv, bias_idx, branch, start, stride):
        def finish(s):
            o, l, m = _band_finish(s, get_vv(), bias_scr[bias_idx])
            if stride == DILATIONS[2]:
                rws = pl.ds(start, ATTN_BLK, stride=D16_PITCH)
            else:
                rws = pl.ds(start, ATTN_BLK, stride=stride) if stride > 1 else pl.ds(start, ATTN_BLK)
            o_scr[branch, rws, :] = o
            l_scr[branch, rws, :] = l
            m_scr[branch, rws, :] = jnp.broadcast_to(m, (ATTN_BLK, HEAD_DIM))
        return (lambda: _dot_nt(get_q(), get_kk()), finish)

    nat_tasks = []
    for r in range(nblk):
        lo = r * ATTN_BLK
        if r == 0:
            get_kk = lambda: jnp.concatenate([kp1[...], k_ref[0, 0, 0, 0:ATTN_BLK, :]], axis=0)
            get_vv = lambda: jnp.concatenate([vp1[...], v_ref[0, 0, 0, 0:ATTN_BLK, :]], axis=0)
        else:
            get_kk = lambda lo=lo: k_ref[0, 0, 0, lo - ATTN_BLK:lo + ATTN_BLK, :]
            get_vv = lambda lo=lo: v_ref[0, 0, 0, lo - ATTN_BLK:lo + ATTN_BLK, :]
        nat_tasks.append(attn_task(lambda lo=lo: q_ref[0, 0, 0, lo:lo + ATTN_BLK, :], get_kk, get_vv,
                                   1 if r == 0 else 0, 0, lo, 1))

    d4, d16 = DILATIONS[1], DILATIONS[2]
    d4_tasks = {}
    for c in range(d4):
        for a in range(nblk // d4):
            lo = a * ATTN_BLK
            d4_tasks[c, a] = attn_task(lambda c=c, lo=lo: q4[c, lo:lo + ATTN_BLK, :],
                                       lambda c=c, lo=lo: k4[c, lo:lo + 2 * ATTN_BLK, :],
                                       lambda c=c, lo=lo: v4[c, lo:lo + 2 * ATTN_BLK, :],
                                       1 if a == 0 else 0, 1, lo * d4 + c, d4)
    d16_tasks = [attn_task(lambda c=c: q16[c], lambda c=c: k16[c], lambda c=c: v16[c], 1, 2, c, d16)
                 for c in range(d16)]

    def merge_task(r):
        def finish(_):
            rows = slice(r * ATTN_BLK, (r + 1) * ATTN_BLK)

            def wide(scr):
                per = ATTN_BLK // d16
                return jnp.concatenate([scr[2, (r * per + k) * D16_PITCH:(r * per + k) * D16_PITCH + d16, :]
                                        for k in range(per)], axis=0)

            m0, m1, m2 = m_scr[0, rows, :], m_scr[1, rows, :], wide(m_scr)
            mx = jnp.maximum(jnp.maximum(m0, m1), m2)
            w0, w1, w2 = jnp.exp2(m0 - mx), jnp.exp2(m1 - mx), jnp.exp2(m2 - mx)
            num = w0 * o_scr[0, rows, :] + w1 * o_scr[1, rows, :] + w2 * wide(o_scr)
            den = w0 * l_scr[0, rows, :] + w1 * l_scr[1, rows, :] + w2 * wide(l_scr)
            o_ref[0, rows, :] = (num * (1.0 / den)).astype(o_ref.dtype)
        return (lambda: None, finish)

    tasks = perm_tasks + d16_tasks
    per_q = nblk // d4
    for a in range(nblk // d4):
        tasks += [d4_tasks[c, a] for c in range(d4)] + nat_tasks[a * per_q:(a + 1) * per_q]
        tasks += [merge_task(r) for r in range(a * per_q, (a + 1) * per_q)]
    _run_tasks(tasks)

    kp1[...] = k_ref[0, 0, 0, SPAN - ATTN_BLK:SPAN, :]
    vp1[...] = v_ref[0, 0, 0, SPAN - ATTN_BLK:SPAN, :]
    for buf in (k4, v4, k16, v16):
        n = buf.shape[1]
        buf[:, 0:ATTN_BLK, :] = buf[:, n - ATTN_BLK:n, :]


def _dilated_attention(main, bsz, seq):
    nspan = seq // SPAN
    d4, d16 = DILATIONS[1], DILATIONS[2]
    blk = (1, 1, 1, SPAN, HEAD_DIM)
    spec = lambda slot: pl.BlockSpec(blk, lambda b, h, n, s=slot: (s, b, h, n, 0))
    cls = lambda dil, halo: pltpu.VMEM((dil, halo + SPAN // dil, HEAD_DIM), BF16)
    return pl.pallas_call(
        _attn_kernel,
        grid=(bsz, N_HEADS, nspan),
        in_specs=[spec(_SLOT_Q), spec(_SLOT_K), spec(_SLOT_V)],
        out_specs=pl.BlockSpec((1, SPAN, HEAD_DIM), lambda b, h, n: (b, n, h)),
        out_shape=jax.ShapeDtypeStruct((bsz, seq, GROUP_COLS), BF16),
        scratch_shapes=[cls(d4, 0), cls(d4, ATTN_BLK), cls(d4, ATTN_BLK),
                        cls(d16, 0), cls(d16, ATTN_BLK), cls(d16, ATTN_BLK),
                        pltpu.VMEM((ATTN_BLK, HEAD_DIM), BF16), pltpu.VMEM((ATTN_BLK, HEAD_DIM), BF16),
                        pltpu.VMEM((2, ATTN_BLK, 2 * ATTN_BLK), F32),
                        ] + [pltpu.VMEM((3, SPAN // d16 * D16_PITCH, HEAD_DIM), F32)] * 3
                        + [pltpu.VMEM((SPAN, HEAD_DIM), F32)] * 3,
        compiler_params=_params(("arbitrary", "arbitrary", "arbitrary")),
        name="dilated_attn",
    )(main, main, main)


def _split3(x):
    hi = x.astype(BF16)
    r1 = x - hi.astype(F32)
    mid = r1.astype(BF16)
    lo = (r1 - mid.astype(F32)).astype(BF16)
    return hi, mid, lo


def _hgrn_chunks(qs, vs, gates, gs, lbs, nw, sts, consts, cs_scr, n_sub):
    tril3, xor_ts, causal, lane_bits, levels = consts
    c_rows = HG_CHUNK
    heads = range(len(qs))

    bs, css = [], []
    for h in heads:
        sg = 1.0 / (1.0 + jnp.exp(-gs[h]))
        f = lbs[h] + (1.0 - lbs[h]) * sg
        l2k = jnp.log2(1.0 - f)
        hi, mid, lo = _split3(jnp.log2(f))
        b = jnp.dot(tril3, jnp.concatenate([hi, mid, lo], axis=0), preferred_element_type=F32)
        bs.append(b)
        css.append(b - l2k)
        cs_scr[h] = css[h]
    b_lasts = [b[c_rows - 1:c_rows, :] for b in bs]

    a_mats = [None for _ in heads]
    for sz in levels:
        zero = jnp.zeros((sz, HEAD_DIM), F32)
        for h in heads:
            q, b, cs = qs[h], bs[h], css[h]
            qparts, kparts = [], []
            for p in range(c_rows // (2 * sz)):
                l0, r0, r1 = p * 2 * sz, p * 2 * sz + sz, (p + 1) * 2 * sz
                bref = b[r0 - 1:r0, :]
                kparts += [jnp.exp2(bref - cs[l0:r0, :]), zero]
                qparts += [zero, q[r0:r1, :] * jnp.exp2(b[r0:r1, :] - bref)]
            al = _dot_nt(jnp.concatenate(qparts, axis=0).astype(BF16),
                         jnp.concatenate(kparts, axis=0).astype(BF16))
            a_mats[h] = al if a_mats[h] is None else jnp.where(xor_ts < sz, a_mats[h], al)

    for h in heads:
        q, b, cs = qs[h], bs[h], css[h]
        blocks = []
        for i in range(c_rows // HG_SUB):
            rs = slice(i * HG_SUB, (i + 1) * HG_SUB)
            bb, qq = b[rs, :], q[rs, :]
            cols = []
            for s in range(HG_SUB):
                p = jnp.exp2(bb - cs_scr[h, i * HG_SUB + s:i * HG_SUB + s + 1, :]) * qq
                cols.append(jnp.broadcast_to(jnp.sum(p, axis=-1, keepdims=True), (HG_SUB, c_rows)))
            for bit in lane_bits:
                cols = [jnp.where(bit, cols[j + 1], cols[j]) for j in range(0, len(cols), 2)]
            blocks.append(cols[0])
        diag = jnp.where(causal, jnp.concatenate(blocks, axis=0), 0.0)
        a_mats[h] = jnp.where(xor_ts < HG_SUB, diag, a_mats[h]).astype(BF16)

    os_ = [jnp.dot(a_mats[h], vs[h], preferred_element_type=F32) for h in heads]

    upds = []
    for h in heads:
        kdec = jnp.exp2(b_lasts[h] - css[h]).astype(BF16)
        upds.append(jnp.dot(vs[h].astype(F32).T.astype(BF16), kdec, preferred_element_type=F32))

    sts = list(sts)
    for j in range(n_sub):
        for hd in range(len(sts)):
            u = hd * n_sub + j
            os_[u] = os_[u] + _dot_nt((qs[u] * jnp.exp2(bs[u])).astype(BF16), sts[hd].astype(BF16))
            sts[hd] = sts[hd] * jnp.exp2(b_lasts[u]) + upds[u]

    ys = [(_rms_rows(os_[h]) * nw * gates[h].astype(F32)).astype(BF16) for h in heads]
    return ys, sts


def _hgrn_kernel(q_ref, i_ref, g_ref, f_ref, lbl_ref, nw_ref, o_ref, st_scr, cs_scr):
    c_rows = HG_CHUNK
    nh = q_ref.shape[2]

    @pl.when(pl.program_id(2) == 0)
    def _():
        st_scr[...] = jnp.zeros_like(st_scr)

    lg = lbl_ref[...]
    e = jnp.exp(lg - jnp.max(lg, axis=0, keepdims=True))
    lb_all = e[0:1, :] / jnp.sum(e, axis=0, keepdims=True)

    ti = lax.broadcasted_iota(jnp.int32, (c_rows, c_rows), 0)
    si = lax.broadcasted_iota(jnp.int32, (c_rows, c_rows), 1)
    tril = (ti >= si).astype(BF16)
    levels = []
    sz = HG_SUB
    while sz < c_rows:
        levels.append(sz)
        sz *= 2
    lane = lax.broadcasted_iota(jnp.int32, (HG_SUB, c_rows), 1)
    lane_bits = tuple((lane & (1 << k)) != 0 for k in range(int(math.log2(HG_SUB))))
    consts = (jnp.concatenate([tril, tril, tril], axis=1), ti ^ si, ti >= si, lane_bits, tuple(levels))

    n_sub = HG_CHUNKS_PER_ITER

    def chunk(ci, carry):
        r0 = pl.multiple_of(ci * (n_sub * c_rows), n_sub * c_rows)
        units = [(h, pl.ds(r0 + j * c_rows, c_rows)) for h in range(nh) for j in range(n_sub)]
        ys, sts = _hgrn_chunks([q_ref[0, 0, h, rows, :].astype(F32) for h, rows in units],
                               [i_ref[0, 0, h, rows, :] for h, rows in units],
                               [g_ref[0, 0, h, rows, :] for h, rows in units],
                               [f_ref[0, h, rows, :] for h, rows in units],
                               [lb_all[:, h * HEAD_DIM:(h + 1) * HEAD_DIM] for h, _ in units],
                               nw_ref[...], [st_scr[h] for h in range(nh)], consts, cs_scr, n_sub)
        for h in range(nh):
            st_scr[h] = sts[h]
        for (h, rows), y in zip(units, ys):
            o_ref[0, rows, h * HEAD_DIM:(h + 1) * HEAD_DIM] = y
        return carry

    lax.fori_loop(0, o_ref.shape[1] // (n_sub * c_rows), chunk, 0)


def _hgrn2(main, gf, lb_logits, hg_norm_w, bsz, seq):
    tt = HG_TOKENS
    nh = HG_HEADS_PER_STEP
    blk = (1, 1, nh, tt, HEAD_DIM)
    spec = lambda slot: pl.BlockSpec(blk, lambda b, h, n, s=slot: (s, b, h, n, 0))
    return pl.pallas_call(
        _hgrn_kernel,
        grid=(bsz, N_HEADS // nh, seq // tt),
        in_specs=[spec(_SLOT_HQ), spec(_SLOT_HI), spec(_SLOT_HG),
                  pl.BlockSpec((1, nh, tt, HEAD_DIM), lambda b, h, n: (b, h, n, 0)),
                  pl.BlockSpec((lb_logits.shape[0], nh * HEAD_DIM), lambda b, h, n: (0, h)),
                  pl.BlockSpec((1, HEAD_DIM), lambda b, h, n: (0, 0))],
        out_specs=pl.BlockSpec((1, tt, nh * HEAD_DIM), lambda b, h, n: (b, n, h)),
        out_shape=jax.ShapeDtypeStruct((bsz, seq, GROUP_COLS), BF16),
        scratch_shapes=[pltpu.VMEM((nh, HEAD_DIM, HEAD_DIM), F32),
                        pltpu.VMEM((nh * HG_CHUNKS_PER_ITER, HG_CHUNK, HEAD_DIM), F32)],
        compiler_params=_params(("arbitrary", "arbitrary", "arbitrary")),
        name="hgrn2",
    )(main, main, main, gf, lb_logits, hg_norm_w.reshape(1, HEAD_DIM))


def _outproj_kernel(x_ref, a_ref, g_ref, mod_ref, aw_ref, w_ref, o_ref):
    half = a_ref.shape[1]
    an = (_rms_rows(a_ref[...].astype(F32)) * aw_ref[...]).astype(BF16)
    mix = (jnp.dot(an, w_ref[0:half, :], preferred_element_type=F32)
           + jnp.dot(g_ref[...], w_ref[half:, :], preferred_element_type=F32))
    o_ref[...] = x_ref[...] + mod_ref[0, 2:3, :] * mix


def _out_projection(x2, attn2, hg2, mod3, attn_norm_w, w_out_bf, seq):
    t, d = x2.shape
    half = attn2.shape[1]
    tm = OUTPROJ_ROWS
    rpb = seq // tm
    row = lambda i: (i, 0)
    return pl.pallas_call(
        _outproj_kernel,
        grid=(t // tm,),
        in_specs=[pl.BlockSpec((tm, d), row),
                  pl.BlockSpec((tm, half), row),
                  pl.BlockSpec((tm, half), row),
                  pl.BlockSpec((1, 6, d), lambda i: (i // rpb, 0, 0)),
                  pl.BlockSpec((1, half), lambda i: (0, 0)),
                  pl.BlockSpec((2 * half, d), lambda i: (0, 0))],
        out_specs=pl.BlockSpec((tm, d), row),
        out_shape=jax.ShapeDtypeStruct((t, d), F32),
        compiler_params=_params(("arbitrary",)),
        name="out_proj",
    )(x2, attn2, hg2, mod3, attn_norm_w.reshape(1, half), w_out_bf)


def _mlp_kernel(x_ref, mod_ref, nw_ref, w1_ref, w2_ref, o_ref, h_scr, acc_scr):
    k = pl.program_id(1)

    @pl.when(k == 0)
    def _():
        _norm_modulate(x_ref, h_scr, nw_ref[...], mod_ref[0, 4:5, :], mod_ref[0, 3:4, :])

    last = pl.num_programs(1) - 1

    def chunk(first, final):
        a = jnp.dot(h_scr[...], w1_ref[...], preferred_element_type=F32)
        a = jnp.square(jnp.maximum(a, 0.0)).astype(BF16)
        part = jnp.dot(a, w2_ref[...], preferred_element_type=F32)
        if first:
            acc_scr[...] = part
        elif final:
            o_ref[...] = x_ref[...] + mod_ref[0, 5:6, :] * (acc_scr[...] + part)
        else:
            acc_scr[...] += part

    pl.when(k == 0)(lambda: chunk(True, False))
    pl.when((k > 0) & (k < last))(lambda: chunk(False, False))
    pl.when(k == last)(lambda: chunk(False, True))


def _mlp(x2, mod3, norm2_w, w1_bf, w2_bf, seq):
    t, d = x2.shape
    dff = w1_bf.shape[1]
    tm, tf = MLP_ROWS, MLP_FF_COLS
    rpb = seq // tm
    row = lambda i, k: (i, 0)
    return pl.pallas_call(
        _mlp_kernel,
        grid=(t // tm, dff // tf),
        in_specs=[pl.BlockSpec((tm, d), row),
                  pl.BlockSpec((1, 6, d), lambda i, k: (i // rpb, 0, 0)),
                  pl.BlockSpec((1, d), lambda i, k: (0, 0)),
                  pl.BlockSpec((d, tf), lambda i, k: (0, k)),
                  pl.BlockSpec((tf, d), lambda i, k: (k, 0))],
        out_specs=pl.BlockSpec((tm, d), row),
        out_shape=jax.ShapeDtypeStruct((t, d), F32),
        scratch_shapes=[pltpu.VMEM((tm, d), BF16), pltpu.VMEM((tm, d), F32)],
        compiler_params=_params(("arbitrary", "arbitrary")),
        name="mlp",
    )(x2, mod3, norm2_w.reshape(1, d), w1_bf, w2_bf)


def kernel(x, c, positions, norm1_w, w_ada, b_ada, w_in, q_norm_w, k_norm_w, attn_out_norm_w,
           hg_lb_logits, hg_norm_w, w_out, norm2_w, w_ff1, w_ff2):
    bsz, seq, d = x.shape
    assert w_ada.shape[0] == 1, "single-layer block"
    assert seq % SPAN == 0 and d == 2 * GROUP_COLS and w_in.shape[2] == _N_GROUPS * GROUP_COLS
    x2 = x.reshape(bsz * seq, d)

    mod3 = _ada_mod(c, w_ada[0], b_ada[0]).reshape(bsz, 6, d)
    tabs = _rope_tables(positions)

    main, gf = _in_projection(x2, mod3, norm1_w[0], w_in[0], q_norm_w[0], k_norm_w[0], tabs, bsz, seq)
    attn = _dilated_attention(main, bsz, seq)
    hg = _hgrn2(main, gf, hg_lb_logits, hg_norm_w[0], bsz, seq)

    x1 = _out_projection(x2, attn.reshape(bsz * seq, GROUP_COLS), hg.reshape(bsz * seq, GROUP_COLS), mod3,
                         attn_out_norm_w[0], w_out[0].astype(BF16), seq)
    out = _mlp(x1, mod3, norm2_w[0], w_ff1[0].astype(BF16), w_ff2[0].astype(BF16), seq)
    return out.reshape(bsz, seq, d)
```

```python
import math

import jax
import jax.numpy as jnp
from jax import lax
from jax.experimental import pallas as pl
from jax.experimental.pallas import tpu as pltpu

F32 = jnp.float32
BF16 = jnp.bfloat16

HEAD_DIM = 128
N_HEADS = 8
GROUP_COLS = N_HEADS * HEAD_DIM
ROPE_THETA = 500000.0
ROPE_DIM = HEAD_DIM // 4
ROPE_HALF = ROPE_DIM // 2
DILATIONS = (1, 4, 16)
ATTN_BLK = 128
SPAN = ATTN_BLK * DILATIONS[-1]
PERM_ROWS = 256
D16_PITCH = 20
HG_CHUNK = 128
HG_SUB = 8
HG_HEADS_PER_STEP = 8
HG_CHUNKS_PER_ITER = 2
EPS = 1e-6
LOG2E = math.log2(math.e)

ADA_COLS = 1024
ROPE_TOKENS = 2048
INPROJ_ROWS = 1024
OUTPROJ_ROWS = 512
MLP_ROWS, MLP_FF_COLS = 512, 1024
HG_TOKENS = 2048
VMEM_LIMIT = 56 * 1024 * 1024

_N_GROUPS = 7
_SLOT_Q, _SLOT_K, _SLOT_V, _SLOT_HQ, _SLOT_HI, _SLOT_HG = range(6)


def _params(sem, vmem=VMEM_LIMIT):
    return pltpu.CompilerParams(dimension_semantics=sem, vmem_limit_bytes=vmem)


def _silu(x):
    return x * (0.5 * jnp.tanh(0.5 * x) + 0.5)


def _rms_rows(x, eps=EPS):
    return x * lax.rsqrt(jnp.mean(x * x, axis=-1, keepdims=True) + eps)


NORM_ROWS = 32


def _norm_modulate(x_ref, h_ref, gain, scale, shift):
    g = gain * (1.0 + scale)
    for c in range(x_ref.shape[0] // NORM_ROWS):
        rows = slice(c * NORM_ROWS, (c + 1) * NORM_ROWS)
        h_ref[rows, :] = (_rms_rows(x_ref[rows, :]) * g + shift).astype(BF16)


def _dot_nt(a, b):
    return lax.dot_general(a, b, (((1,), (1,)), ((), ())), preferred_element_type=F32)


def _ada_kernel(c_ref, w_ref, b_ref, o_ref):
    sc = _silu(c_ref[...]).astype(BF16)
    o_ref[...] = jnp.dot(sc, w_ref[...].astype(BF16), preferred_element_type=F32) + b_ref[...]


def _ada_mod(c, w_ada, b_ada):
    bsz, d = c.shape
    n = w_ada.shape[1]
    tn = ADA_COLS
    rows = 8
    c8 = jnp.pad(c, ((0, rows - bsz), (0, 0)))
    out = pl.pallas_call(
        _ada_kernel,
        grid=(n // tn,),
        in_specs=[pl.BlockSpec((rows, d), lambda j: (0, 0)),
                  pl.BlockSpec((d, tn), lambda j: (0, j)),
                  pl.BlockSpec((1, tn), lambda j: (0, j))],
        out_specs=pl.BlockSpec((rows, tn), lambda j: (0, j)),
        out_shape=jax.ShapeDtypeStruct((rows, n), F32),
        compiler_params=_params(("arbitrary",)),
        name="ada_mod",
    )(c8, w_ada, b_ada.reshape(1, n))
    return out[:bsz]


def _rope_kernel(pos_ref, invf_ref, cos_ref, sin_ref):
    ang = pos_ref[...] * invf_ref[...]
    cos_ref[...] = jnp.cos(ang)
    sin_ref[...] = jnp.sin(ang)


def _rope_tables(positions):
    t = positions.size
    ts = ROPE_TOKENS
    inv_freq = ROPE_THETA ** (-(jnp.arange(ROPE_HALF, dtype=F32) * 2.0) / ROPE_DIM)
    invf = jnp.broadcast_to(inv_freq[:, None], (ROPE_HALF, ts))
    pos = positions.astype(F32).reshape(1, t)
    tab = jax.ShapeDtypeStruct((ROPE_HALF, t), F32)
    spec = pl.BlockSpec((ROPE_HALF, ts), lambda i: (0, i))
    return pl.pallas_call(
        _rope_kernel,
        grid=(t // ts,),
        in_specs=[pl.BlockSpec((1, ts), lambda i: (0, i)),
                  pl.BlockSpec((ROPE_HALF, ts), lambda i: (0, 0))],
        out_specs=[spec, spec],
        out_shape=[tab, tab],
        compiler_params=_params(("arbitrary",)),
        name="rope_tables",
    )(pos, invf)


_PROJ_COLS = 2 * HEAD_DIM
_QK_TOKENS = 256


def _inproj_kernel(x_ref, mod_ref, n1w_ref, w_ref, qg_ref, kg_ref, cos_ref, sin_ref,
                   main_ref, gf_ref, h_scr):
    j = pl.program_id(1)

    def heads(fn):
        for hp in range(GROUP_COLS // _PROJ_COLS):
            res = jnp.dot(h_scr[...], w_ref[:, hp * _PROJ_COLS:(hp + 1) * _PROJ_COLS],
                          preferred_element_type=F32)
            for hh in range(_PROJ_COLS // HEAD_DIM):
                fn(hp * (_PROJ_COLS // HEAD_DIM) + hh, res[:, hh * HEAD_DIM:(hh + 1) * HEAD_DIM])

    def qk_step(gain_ref, with_norm):
        if with_norm:
            _norm_modulate(x_ref, h_scr, n1w_ref[...], mod_ref[0, 1:2, :], mod_ref[0, 0:1, :])
        nt = _QK_TOKENS
        gain = jnp.concatenate([gain_ref[...]] * (nt // HEAD_DIM), axis=1)
        w_t = w_ref[...].T
        for part in range(h_scr.shape[0] // nt):
            toks = slice(part * nt, (part + 1) * nt)
            rt = _dot_nt(w_t, h_scr[toks, :])
            cos, sin = cos_ref[:, toks], sin_ref[:, toks]
            for h in range(N_HEADS):
                r = rt[h * HEAD_DIM:(h + 1) * HEAD_DIM, :]
                y = r * lax.rsqrt(jnp.mean(r * r, axis=0, keepdims=True) + EPS) * gain
                x1, x2 = y[0:ROPE_HALF, :], y[ROPE_HALF:ROPE_DIM, :]
                y = jnp.concatenate([x1 * cos - x2 * sin, x2 * cos + x1 * sin, y[ROPE_DIM:, :]], axis=0)
                main_ref[0, 0, h, toks, :] = y.T.astype(BF16)

    pl.when(j == _SLOT_Q)(lambda: qk_step(qg_ref, True))
    pl.when(j == _SLOT_K)(lambda: qk_step(kg_ref, False))

    @pl.when((j == _SLOT_V) | (j == _SLOT_HI))
    def _():
        def fn(h, r):
            main_ref[0, 0, h] = r.astype(BF16)
        heads(fn)

    @pl.when(j == _SLOT_HQ)
    def _():
        def fn(h, r):
            main_ref[0, 0, h] = (_silu(r) * (HEAD_DIM ** -0.5)).astype(BF16)
        heads(fn)

    @pl.when(j == _SLOT_HG)
    def _():
        def fn(h, r):
            main_ref[0, 0, h] = _silu(r).astype(BF16)
        heads(fn)

    @pl.when(j == _N_GROUPS - 1)
    def _():
        def fn(h, r):
            gf_ref[0, h] = r
        heads(fn)


def _in_projection(x2, mod3, norm1_w, w_in, qnw, knw, tabs, bsz, seq):
    t, d = x2.shape
    tm = INPROJ_ROWS
    rpb = seq // tm
    w_bf = w_in.astype(BF16)
    lanes = lambda g: jnp.broadcast_to(g[:, None], (HEAD_DIM, HEAD_DIM))
    qg = lanes(qnw * (HEAD_DIM ** -0.5 * LOG2E))
    kg = lanes(knw)

    def wcol(i, j):
        return (0, jnp.where(j <= 3, j, jnp.where(j == 4, 5, jnp.where(j == 5, 6, 4))))

    row = lambda i, j: (i, 0)
    const = lambda i, j: (0, 0)
    main_shape = jax.ShapeDtypeStruct((6, bsz, N_HEADS, seq, HEAD_DIM), BF16)
    gf_shape = jax.ShapeDtypeStruct((bsz, N_HEADS, seq, HEAD_DIM), F32)
    return pl.pallas_call(
        _inproj_kernel,
        grid=(t // tm, _N_GROUPS),
        in_specs=[pl.BlockSpec((tm, d), row),
                  pl.BlockSpec((1, 6, d), lambda i, j: (i // rpb, 0, 0)),
                  pl.BlockSpec((1, d), const),
                  pl.BlockSpec((d, GROUP_COLS), wcol),
                  pl.BlockSpec((HEAD_DIM, HEAD_DIM), const),
                  pl.BlockSpec((HEAD_DIM, HEAD_DIM), const),
                  pl.BlockSpec((ROPE_HALF, tm), lambda i, j: (0, i)),
                  pl.BlockSpec((ROPE_HALF, tm), lambda i, j: (0, i))],
        out_specs=[pl.BlockSpec((1, 1, N_HEADS, tm, HEAD_DIM),
                                lambda i, j: (jnp.minimum(j, 5), i // rpb, 0, i % rpb, 0)),
                   pl.BlockSpec((1, N_HEADS, tm, HEAD_DIM), lambda i, j: (i // rpb, 0, i % rpb, 0))],
        out_shape=[main_shape, gf_shape],
        scratch_shapes=[pltpu.VMEM((tm, d), BF16)],
        compiler_params=_params(("arbitrary", "arbitrary")),
        name="in_proj",
    )(x2, mod3, norm1_w.reshape(1, d), w_bf, qg, kg, *tabs)


def _perm_matrix(dil):
    per = PERM_ROWS // dil
    r = lax.broadcasted_iota(jnp.int32, (PERM_ROWS, PERM_ROWS), 0)
    s = lax.broadcasted_iota(jnp.int32, (PERM_ROWS, PERM_ROWS), 1)
    src = (r & (per - 1)) * dil + lax.shift_right_logical(r, int(math.log2(per)))
    return (s == src).astype(BF16)


def _band_finish(s, vv, bias):
    s = s + bias
    m = jnp.max(s, axis=-1, keepdims=True)
    p = jnp.exp2((s - m).astype(BF16))
    ones = jnp.ones((vv.shape[0], HEAD_DIM), BF16)
    ol = jnp.dot(p, jnp.concatenate([vv, ones], axis=1), preferred_element_type=F32)
    return ol[:, :HEAD_DIM], ol[:, HEAD_DIM:], m


def _run_tasks(tasks):
    for issue, finish in tasks:
        finish(issue())


def _attn_kernel(q_ref, k_ref, v_ref, o_ref, q4, k4, v4, q16, k16, v16, kp1, vp1, bias_scr, o_scr, l_scr,
                 m_scr, qf_scr, kf_scr, vf_scr):
    first = pl.program_id(2) == 0
    nblk = SPAN // ATTN_BLK
    ngrp = SPAN // PERM_ROWS

    @pl.when(first)
    def _():
        kp1[...] = jnp.zeros_like(kp1)
        vp1[...] = jnp.zeros_like(vp1)
        for buf in (k4, v4, k16, v16):
            buf[:, 0:ATTN_BLK, :] = jnp.zeros((buf.shape[0], ATTN_BLK, HEAD_DIM), BF16)

    qi = lax.broadcasted_iota(jnp.int32, (ATTN_BLK, 2 * ATTN_BLK), 0)
    kj = lax.broadcasted_iota(jnp.int32, (ATTN_BLK, 2 * ATTN_BLK), 1)
    dist = ATTN_BLK + qi - kj
    band = (dist >= 0) & (dist <= ATTN_BLK)
    bias_scr[0] = jnp.where(band, 0.0, -jnp.inf)
    first_key = jnp.where(first, ATTN_BLK, 0)
    bias_scr[1] = jnp.where(band & (kj >= first_key), 0.0, -jnp.inf)

    perm_tasks = []

    def gather4(src, dst, off, buf):
        def finish(_):
            buf[...] = src[0, 0, 0].astype(F32)
            for c in range(DILATIONS[1]):
                dst[c, off:off + SPAN // DILATIONS[1], :] = (
                    buf[pl.ds(c, SPAN // DILATIONS[1], stride=DILATIONS[1]), :].astype(BF16))
        return (lambda: None, finish)

    perm_tasks += [gather4(k_ref, k4, ATTN_BLK, kf_scr), gather4(v_ref, v4, ATTN_BLK, vf_scr),
                   gather4(q_ref, q4, 0, qf_scr)]

    for dil, qd, kd, vd in ((DILATIONS[2], q16, k16, v16),):
        perm = _perm_matrix(dil)
        per = PERM_ROWS // dil

        def scatter(dst, y, g, off, dil=dil, per=per):
            for c in range(dil):
                dst[c, off + g * per:off + (g + 1) * per, :] = y[c * per:(c + 1) * per, :]

        def rows(g):
            return slice(g * PERM_ROWS, (g + 1) * PERM_ROWS)

        def issue(a, ga, b, gb, perm=perm):
            x = jnp.concatenate([a[0, 0, 0, rows(ga), :], b[0, 0, 0, rows(gb), :]], axis=1)
            return jnp.dot(perm, x, preferred_element_type=F32)

        def finish(y, da, ga, db, gb, off, scatter=scatter):
            y = y.astype(BF16)
            scatter(da, y[:, :HEAD_DIM], ga, off)
            scatter(db, y[:, HEAD_DIM:], gb, off)

        for g in range(ngrp):
            perm_tasks.append((lambda g=g, issue=issue: issue(k_ref, g, v_ref, g),
                               lambda y, g=g, finish=finish, kd=kd, vd=vd: finish(y, kd, g, vd, g, ATTN_BLK)))
        for g in range(0, ngrp, 2):
            perm_tasks.append((lambda g=g, issue=issue: issue(q_ref, g, q_ref, g + 1),
                               lambda y, g=g, finish=finish, qd=qd: finish(y, qd, g, qd, g + 1, 0)))

    def attn_task(get_q, get_kk, get_vv, bias_idx, branch, start, stride):
        def finish(s):
            o, l, m = _band_finish(s, get_vv(), bias_scr[bias_idx])
            if stride == DILATIONS[2]:
                rws = pl.ds(start, ATTN_BLK, stride=D16_PITCH)
            else:
                rws = pl.ds(start, ATTN_BLK, stride=stride) if stride > 1 else pl.ds(start, ATTN_BLK)
            o_scr[branch, rws, :] = o
            l_scr[branch, rws, :] = l
            m_scr[branch, rws, :] = jnp.broadcast_to(m, (ATTN_BLK, HEAD_DIM))
        return (lambda: _dot_nt(get_q(), get_kk()), finish)

    nat_tasks = []
    for r in range(nblk):
        lo = r * ATTN_BLK
        if r == 0:
            get_kk = lambda: jnp.concatenate([kp1[...], k_ref[0, 0, 0, 0:ATTN_BLK, :]], axis=0)
            get_vv = lambda: jnp.concatenate([vp1[...], v_ref[0, 0, 0, 0:ATTN_BLK, :]], axis=0)
        else:
            get_kk = lambda lo=lo: k_ref[0, 0, 0, lo - ATTN_BLK:lo + ATTN_BLK, :]
            get_vv = lambda lo=lo: v_ref[0, 0, 0, lo - ATTN_BLK:lo + ATTN_BLK, :]
        nat_tasks.append(attn_task(lambda lo=lo: q_ref[0, 0, 0, lo:lo + ATTN_BLK, :], get_kk, get_vv,
                                   1 if r == 0 else 0, 0, lo, 1))

    d4, d16 = DILATIONS[1], DILATIONS[2]
    d4_tasks = {}
    for c in range(d4):
        for a in range(nblk // d4):
            lo = a * ATTN_BLK
            d4_tasks[c, a] = attn_task(lambda c=c, lo=lo: q4[c, lo:lo + ATTN_BLK, :],
                                       lambda c=c, lo=lo: k4[c, lo:lo + 2 * ATTN_BLK, :],
                                       lambda c=c, lo=lo: v4[c, lo:lo + 2 * ATTN_BLK, :],
                                       1 if a == 0 else 0, 1, lo * d4 + c, d4)
    d16_tasks = [attn_task(lambda c=c: q16[c], lambda c=c: k16[c], lambda c=c: v16[c], 1, 2, c, d16)
                 for c in range(d16)]

    def merge_task(r):
        def finish(_):
            rows = slice(r * ATTN_BLK, (r + 1) * ATTN_BLK)

            def wide(scr):
                per = ATTN_BLK // d16
                return jnp.concatenate([scr[2, (r * per + k) * D16_PITCH:(r * per + k) * D16_PITCH + d16, :]
                                        for k in range(per)], axis=0)

            m0, m1, m2 = m_scr[0, rows, :], m_scr[1, rows, :], wide(m_scr)
            mx = jnp.maximum(jnp.maximum(m0, m1), m2)
            w0, w1, w2 = jnp.exp2(m0 - mx), jnp.exp2(m1 - mx), jnp.exp2(m2 - mx)
            num = w0 * o_scr[0, rows, :] + w1 * o_scr[1, rows, :] + w2 * wide(o_scr)
            den = w0 * l_scr[0, rows, :] + w1 * l_scr[1, rows, :] + w2 * wide(l_scr)
            o_ref[0, rows, :] = (num * (1.0 / den)).astype(o_ref.dtype)
        return (lambda: None, finish)

    tasks = perm_tasks + d16_tasks
    per_q = nblk // d4
    for a in range(nblk // d4):
        tasks += [d4_tasks[c, a] for c in range(d4)] + nat_tasks[a * per_q:(a + 1) * per_q]
        tasks += [merge_task(r) for r in range(a * per_q, (a + 1) * per_q)]
    _run_tasks(tasks)

    kp1[...] = k_ref[0, 0, 0, SPAN - ATTN_BLK:SPAN, :]
    vp1[...] = v_ref[0, 0, 0, SPAN - ATTN_BLK:SPAN, :]
    for buf in (k4, v4, k16, v16):
        n = buf.shape[1]
        buf[:, 0:ATTN_BLK, :] = buf[:, n - ATTN_BLK:n, :]


def _dilated_attention(main, bsz, seq):
    nspan = seq // SPAN
    d4, d16 = DILATIONS[1], DILATIONS[2]
    blk = (1, 1, 1, SPAN, HEAD_DIM)
    spec = lambda slot: pl.BlockSpec(blk, lambda b, h, n, s=slot: (s, b, h, n, 0))
    cls = lambda dil, halo: pltpu.VMEM((dil, halo + SPAN // dil, HEAD_DIM), BF16)
    return pl.pallas_call(
        _attn_kernel,
        grid=(bsz, N_HEADS, nspan),
        in_specs=[spec(_SLOT_Q), spec(_SLOT_K), spec(_SLOT_V)],
        out_specs=pl.BlockSpec((1, SPAN, HEAD_DIM), lambda b, h, n: (b, n, h)),
        out_shape=jax.ShapeDtypeStruct((bsz, seq, GROUP_COLS), BF16),
        scratch_shapes=[cls(d4, 0), cls(d4, ATTN_BLK), cls(d4, ATTN_BLK),
                        cls(d16, 0), cls(d16, ATTN_BLK), cls(d16, ATTN_BLK),
                        pltpu.VMEM((ATTN_BLK, HEAD_DIM), BF16), pltpu.VMEM((ATTN_BLK, HEAD_DIM), BF16),
                        pltpu.VMEM((2, ATTN_BLK, 2 * ATTN_BLK), F32),
                        ] + [pltpu.VMEM((3, SPAN // d16 * D16_PITCH, HEAD_DIM), F32)] * 3
                        + [pltpu.VMEM((SPAN, HEAD_DIM), F32)] * 3,
        compiler_params=_params(("arbitrary", "arbitrary", "arbitrary")),
        name="dilated_attn",
    )(main, main, main)


def _split3(x):
    hi = x.astype(BF16)
    r1 = x - hi.astype(F32)
    mid = r1.astype(BF16)
    lo = (r1 - mid.astype(F32)).astype(BF16)
    return hi, mid, lo


def _hgrn_chunks(qs, vs, gates, gs, lbs, nw, sts, consts, cs_scr, n_sub):
    tril3, xor_ts, causal, lane_bits, levels = consts
    c_rows = HG_CHUNK
    heads = range(len(qs))

    bs, css = [], []
    for h in heads:
        sg = 1.0 / (1.0 + jnp.exp(-gs[h]))
        f = lbs[h] + (1.0 - lbs[h]) * sg
        l2k = jnp.log2(1.0 - f)
        hi, mid, lo = _split3(jnp.log2(f))
        b = jnp.dot(tril3, jnp.concatenate([hi, mid, lo], axis=0), preferred_element_type=F32)
        bs.append(b)
        css.append(b - l2k)
        cs_scr[h] = css[h]
    b_lasts = [b[c_rows - 1:c_rows, :] for b in bs]

    a_mats = [None for _ in heads]
    for sz in levels:
        zero = jnp.zeros((sz, HEAD_DIM), F32)
        for h in heads:
            q, b, cs = qs[h], bs[h], css[h]
            qparts, kparts = [], []
            for p in range(c_rows // (2 * sz)):
                l0, r0, r1 = p * 2 * sz, p * 2 * sz + sz, (p + 1) * 2 * sz
                bref = b[r0 - 1:r0, :]
                kparts += [jnp.exp2(bref - cs[l0:r0, :]), zero]
                qparts += [zero, q[r0:r1, :] * jnp.exp2(b[r0:r1, :] - bref)]
            al = _dot_nt(jnp.concatenate(qparts, axis=0).astype(BF16),
                         jnp.concatenate(kparts, axis=0).astype(BF16))
            a_mats[h] = al if a_mats[h] is None else jnp.where(xor_ts < sz, a_mats[h], al)

    for h in heads:
        q, b, cs = qs[h], bs[h], css[h]
        blocks = []
        for i in range(c_rows // HG_SUB):
            rs = slice(i * HG_SUB, (i + 1) * HG_SUB)
            bb, qq = b[rs, :], q[rs, :]
            cols = []
            for s in range(HG_SUB):
                p = jnp.exp2(bb - cs_scr[h, i * HG_SUB + s:i * HG_SUB + s + 1, :]) * qq
                cols.append(jnp.broadcast_to(jnp.sum(p, axis=-1, keepdims=True), (HG_SUB, c_rows)))
            for bit in lane_bits:
                cols = [jnp.where(bit, cols[j + 1], cols[j]) for j in range(0, len(cols), 2)]
            blocks.append(cols[0])
        diag = jnp.where(causal, jnp.concatenate(blocks, axis=0), 0.0)
        a_mats[h] = jnp.where(xor_ts < HG_SUB, diag, a_mats[h]).astype(BF16)

    os_ = [jnp.dot(a_mats[h], vs[h], preferred_element_type=F32) for h in heads]

    upds = []
    for h in heads:
        kdec = jnp.exp2(b_lasts[h] - css[h]).astype(BF16)
        upds.append(jnp.dot(vs[h].astype(F32).T.astype(BF16), kdec, preferred_element_type=F32))

    sts = list(sts)
    for j in range(n_sub):
        for hd in range(len(sts)):
            u = hd * n_sub + j
            os_[u] = os_[u] + _dot_nt((qs[u] * jnp.exp2(bs[u])).astype(BF16), sts[hd].astype(BF16))
            sts[hd] = sts[hd] * jnp.exp2(b_lasts[u]) + upds[u]

    ys = [(_rms_rows(os_[h]) * nw * gates[h].astype(F32)).astype(BF16) for h in heads]
    return ys, sts


def _hgrn_kernel(q_ref, i_ref, g_ref, f_ref, lbl_ref, nw_ref, o_ref, st_scr, cs_scr):
    c_rows = HG_CHUNK
    nh = q_ref.shape[2]

    @pl.when(pl.program_id(2) == 0)
    def _():
        st_scr[...] = jnp.zeros_like(st_scr)

    lg = lbl_ref[...]
    e = jnp.exp(lg - jnp.max(lg, axis=0, keepdims=True))
    lb_all = e[0:1, :] / jnp.sum(e, axis=0, keepdims=True)

    ti = lax.broadcasted_iota(jnp.int32, (c_rows, c_rows), 0)
    si = lax.broadcasted_iota(jnp.int32, (c_rows, c_rows), 1)
    tril = (ti >= si).astype(BF16)
    levels = []
    sz = HG_SUB
    while sz < c_rows:
        levels.append(sz)
        sz *= 2
    lane = lax.broadcasted_iota(jnp.int32, (HG_SUB, c_rows), 1)
    lane_bits = tuple((lane & (1 << k)) != 0 for k in range(int(math.log2(HG_SUB))))
    consts = (jnp.concatenate([tril, tril, tril], axis=1), ti ^ si, ti >= si, lane_bits, tuple(levels))

    n_sub = HG_CHUNKS_PER_ITER

    def chunk(ci, carry):
        r0 = pl.multiple_of(ci * (n_sub * c_rows), n_sub * c_rows)
        units = [(h, pl.ds(r0 + j * c_rows, c_rows)) for h in range(nh) for j in range(n_sub)]
        ys, sts = _hgrn_chunks([q_ref[0, 0, h, rows, :].astype(F32) for h, rows in units],
                               [i_ref[0, 0, h, rows, :] for h, rows in units],
                               [g_ref[0, 0, h, rows, :] for h, rows in units],
                               [f_ref[0, h, rows, :] for h, rows in units],
                               [lb_all[:, h * HEAD_DIM:(h + 1) * HEAD_DIM] for h, _ in units],
                               nw_ref[...], [st_scr[h] for h in range(nh)], consts, cs_scr, n_sub)
        for h in range(nh):
            st_scr[h] = sts[h]
        for (h, rows), y in zip(units, ys):
            o_ref[0, rows, h * HEAD_DIM:(h + 1) * HEAD_DIM] = y
        return carry

    lax.fori_loop(0, o_ref.shape[1] // (n_sub * c_rows), chunk, 0)


def _hgrn2(main, gf, lb_logits, hg_norm_w, bsz, seq):
    tt = HG_TOKENS
    nh = HG_HEADS_PER_STEP
    blk = (1, 1, nh, tt, HEAD_DIM)
    spec = lambda slot: pl.BlockSpec(blk, lambda b, h, n, s=slot: (s, b, h, n, 0))
    return pl.pallas_call(
        _hgrn_kernel,
        grid=(bsz, N_HEADS // nh, seq // tt),
        in_specs=[spec(_SLOT_HQ), spec(_SLOT_HI), spec(_SLOT_HG),
                  pl.BlockSpec((1, nh, tt, HEAD_DIM), lambda b, h, n: (b, h, n, 0)),
                  pl.BlockSpec((lb_logits.shape[0], nh * HEAD_DIM), lambda b, h, n: (0, h)),
                  pl.BlockSpec((1, HEAD_DIM), lambda b, h, n: (0, 0))],
        out_specs=pl.BlockSpec((1, tt, nh * HEAD_DIM), lambda b, h, n: (b, n, h)),
        out_shape=jax.ShapeDtypeStruct((bsz, seq, GROUP_COLS), BF16),
        scratch_shapes=[pltpu.VMEM((nh, HEAD_DIM, HEAD_DIM), F32),
                        pltpu.VMEM((nh * HG_CHUNKS_PER_ITER, HG_CHUNK, HEAD_DIM), F32)],
        compiler_params=_params(("arbitrary", "arbitrary", "arbitrary")),
        name="hgrn2",
    )(main, main, main, gf, lb_logits, hg_norm_w.reshape(1, HEAD_DIM))


def _outproj_kernel(x_ref, a_ref, g_ref, mod_ref, aw_ref, w_ref, o_ref):
    half = a_ref.shape[1]
    an = (_rms_rows(a_ref[...].astype(F32)) * aw_ref[...]).astype(BF16)
    mix = (jnp.dot(an, w_ref[0:half, :], preferred_element_type=F32)
           + jnp.dot(g_ref[...], w_ref[half:, :], preferred_element_type=F32))
    o_ref[...] = x_ref[...] + mod_ref[0, 2:3, :] * mix


def _out_projection(x2, attn2, hg2, mod3, attn_norm_w, w_out_bf, seq):
    t, d = x2.shape
    half = attn2.shape[1]
    tm = OUTPROJ_ROWS
    rpb = seq // tm
    row = lambda i: (i, 0)
    return pl.pallas_call(
        _outproj_kernel,
        grid=(t // tm,),
        in_specs=[pl.BlockSpec((tm, d), row),
                  pl.BlockSpec((tm, half), row),
                  pl.BlockSpec((tm, half), row),
                  pl.BlockSpec((1, 6, d), lambda i: (i // rpb, 0, 0)),
                  pl.BlockSpec((1, half), lambda i: (0, 0)),
                  pl.BlockSpec((2 * half, d), lambda i: (0, 0), pipeline_mode=pl.Buffered(1))],
        out_specs=pl.BlockSpec((tm, d), row),
        out_shape=jax.ShapeDtypeStruct((t, d), F32),
        compiler_params=_params(("arbitrary",)),
        name="out_proj",
    )(x2, attn2, hg2, mod3, attn_norm_w.reshape(1, half), w_out_bf)


def _mlp_kernel(x_ref, mod_ref, nw_ref, w1_ref, w2_ref, o_ref, h_scr, acc_scr):
    k = pl.program_id(1)

    @pl.when(k == 0)
    def _():
        _norm_modulate(x_ref, h_scr, nw_ref[...], mod_ref[0, 4:5, :], mod_ref[0, 3:4, :])

    last = pl.num_programs(1) - 1

    def chunk(first, final):
        a = jnp.dot(h_scr[...], w1_ref[...], preferred_element_type=F32)
        a = jnp.square(jnp.maximum(a, 0.0)).astype(BF16)
        part = jnp.dot(a, w2_ref[...], preferred_element_type=F32)
        if first:
            acc_scr[...] = part
        elif final:
            o_ref[...] = x_ref[...] + mod_ref[0, 5:6, :] * (acc_scr[...] + part)
        else:
            acc_scr[...] += part

    pl.when(k == 0)(lambda: chunk(True, False))
    pl.when((k > 0) & (k < last))(lambda: chunk(False, False))
    pl.when(k == last)(lambda: chunk(False, True))


def _mlp(x2, mod3, norm2_w, w1_bf, w2_bf, seq):
    t, d = x2.shape
    dff = w1_bf.shape[1]
    tm, tf = MLP_ROWS, MLP_FF_COLS
    rpb = seq // tm
    row = lambda i, k: (i, 0)
    return pl.pallas_call(
        _mlp_kernel,
        grid=(t // tm, dff // tf),
        in_specs=[pl.BlockSpec((tm, d), row),
                  pl.BlockSpec((1, 6, d), lambda i, k: (i // rpb, 0, 0)),
                  pl.BlockSpec((1, d), lambda i, k: (0, 0)),
                  pl.BlockSpec((d, tf), lambda i, k: (0, k)),
                  pl.BlockSpec((tf, d), lambda i, k: (k, 0))],
        out_specs=pl.BlockSpec((tm, d), row),
        out_shape=jax.ShapeDtypeStruct((t, d), F32),
        scratch_shapes=[pltpu.VMEM((tm, d), BF16), pltpu.VMEM((tm, d), F32)],
        compiler_params=_params(("arbitrary", "arbitrary")),
        name="mlp",
    )(x2, mod3, norm2_w.reshape(1, d), w1_bf, w2_bf)


def kernel(x, c, positions, norm1_w, w_ada, b_ada, w_in, q_norm_w, k_norm_w, attn_out_norm_w,
           hg_lb_logits, hg_norm_w, w_out, norm2_w, w_ff1, w_ff2):
    bsz, seq, d = x.shape
    assert w_ada.shape[0] == 1, "single-layer block"
    assert seq % SPAN == 0 and d == 2 * GROUP_COLS and w_in.shape[2] == _N_GROUPS * GROUP_COLS
    x2 = x.reshape(bsz * seq, d)

    mod3 = _ada_mod(c, w_ada[0], b_ada[0]).reshape(bsz, 6, d)
    tabs = _rope_tables(positions)

    main, gf = _in_projection(x2, mod3, norm1_w[0], w_in[0], q_norm_w[0], k_norm_w[0], tabs, bsz, seq)
    attn = _dilated_attention(main, bsz, seq)
    hg = _hgrn2(main, gf, hg_lb_logits, hg_norm_w[0], bsz, seq)

    x1 = _out_projection(x2, attn.reshape(bsz * seq, GROUP_COLS), hg.reshape(bsz * seq, GROUP_COLS), mod3,
                         attn_out_norm_w[0], w_out[0].astype(BF16), seq)
    out = _mlp(x1, mod3, norm2_w[0], w_ff1[0].astype(BF16), w_ff2[0].astype(BF16), seq)
    return out.reshape(bsz, seq, d)
```
